```python
import math
import jax, jax.numpy as jnp
from jax import lax
import numpy as np

D_MODEL = 4096
BATCH = 4
SEQ = 2048
DEPTH = 2
DEC_BATCH = 1
DEC_SEQ = 16384
PAST_LEN = 128

HEAD_DIM = 128
GRID_W = 64
A_HEADS = 12
A_KV_HEADS = 4
A_GROUP = A_HEADS // A_KV_HEADS
B_HEADS = 8
C_HEADS = 6
A_Q = A_HEADS * HEAD_DIM
A_KV = A_KV_HEADS * HEAD_DIM
B_W = B_HEADS * HEAD_DIM
C_QK = C_HEADS * 2 * HEAD_DIM
C_V = C_HEADS * 2 * HEAD_DIM
D_MIX = A_Q + B_W + C_V
D_IN = A_Q + 2 * A_KV + 3 * B_W + 2 * C_QK + C_V + D_MIX
SPLITS = list(np.cumsum([A_Q, A_KV, A_KV, B_W, B_W, B_W, C_QK, C_QK, C_V])[:].tolist())
Q_BLOCK = 128
NA_ROWS = 8
NA_COLS = 16
AXIAL_THETA = 10000.0
ROPE_THETA = 500000.0
ROPE_DIMS = HEAD_DIM // 4
EPS = 1e-6
NEG_INF = -1e30

kernel_name = "hybrid_parallel_head_group_encoder"


def _rmsnorm(x, g):
    xf = x.astype(jnp.float32)
    xf = xf * lax.rsqrt(jnp.mean(xf * xf, axis=-1, keepdims=True) + EPS)
    return xf.astype(x.dtype) * g


def _angles(pos, dim, theta):
    inv = jnp.power(theta, -jnp.arange(0, dim, 2, dtype=jnp.float32) / dim)
    return pos.astype(jnp.float32)[:, None] * inv[None, :]


def _rope(x, ang):
    n = ang.shape[-1]
    xf = x.astype(jnp.float32)
    x1, x2 = xf[..., :n], xf[..., n:]
    c = jnp.cos(ang)[:, None, :]
    s = jnp.sin(ang)[:, None, :]
    return jnp.concatenate([x1 * c - x2 * s, x2 * c + x1 * s], axis=-1).astype(x.dtype)


def _axial_rope(x, ang_row, ang_col):
    half = HEAD_DIM // 2
    return jnp.concatenate([_rope(x[..., :half], ang_row), _rope(x[..., half:], ang_col)], axis=-1)


def _partial_rope(x, ang):
    return jnp.concatenate([_rope(x[..., :ROPE_DIMS], ang), x[..., ROPE_DIMS:]], axis=-1)


def _gqa_attention(q, k, v):
    B, L, H, D = q.shape
    nb = L // Q_BLOCK
    scale = 1.0 / math.sqrt(D)
    qb = q.reshape(B, nb, Q_BLOCK, A_KV_HEADS, A_GROUP, D).swapaxes(0, 1)

    def block(qi):
        s = jnp.einsum('bqkgd,bskd->bkgqs', qi, k).astype(jnp.float32) * scale
        p = jax.nn.softmax(s, axis=-1).astype(v.dtype)
        return jnp.einsum('bkgqs,bskd->bqkgd', p, v)

    o = lax.map(block, qb)
    return o.swapaxes(0, 1).reshape(B, L, H * D)


def _neighborhood_attention(q, k, v, rel_bias):
    B, L, H, D = q.shape
    rows = L // GRID_W
    wr = min(NA_ROWS, rows)
    scale = 1.0 / math.sqrt(D)
    r = jnp.arange(rows)
    row_start = jnp.clip(r - wr // 2, 0, rows - wr)
    band = row_start[:, None] + jnp.arange(wr)[None, :]
    c = jnp.arange(GRID_W)
    col_start = jnp.clip(c - NA_COLS // 2, 0, GRID_W - NA_COLS)
    col_ok = (c[None, :] >= col_start[:, None]) & (c[None, :] < col_start[:, None] + NA_COLS)
    qg = q.reshape(B, rows, GRID_W, H, D)
    kg = k.reshape(B, rows, GRID_W, H, D)[:, band]
    vg = v.reshape(B, rows, GRID_W, H, D)[:, band]
    s = jnp.einsum('brqhd,brikhd->brhqik', qg, kg).astype(jnp.float32) * scale
    dr = (band - r[:, None]) + (NA_ROWS - 1)
    dc = jnp.clip(c[None, :] - c[:, None], -(NA_COLS - 1), NA_COLS - 1) + (NA_COLS - 1)
    bias = rel_bias[:, dr[:, None, :, None], dc[None, :, None, :]]
    bias = jnp.transpose(bias, (1, 0, 2, 3, 4)).astype(jnp.float32)
    s = jnp.where(col_ok[:, None, :], s + bias[None], NEG_INF)
    p = jax.nn.softmax(s, axis=(-2, -1)).astype(v.dtype)
    o = jnp.einsum('brhqik,brikhd->brqhd', p, vg)
    return o.reshape(B, L, H * D)


def _diff_attention(q1, q2, k1, k2, v, lam):
    B, L, H, D = q1.shape
    nb = L // Q_BLOCK
    scale = 1.0 / math.sqrt(D)
    q1b = q1.reshape(B, nb, Q_BLOCK, H, D).swapaxes(0, 1)
    q2b = q2.reshape(B, nb, Q_BLOCK, H, D).swapaxes(0, 1)

    def block(qs):
        q1i, q2i = qs
        s1 = jnp.einsum('bqhd,bshd->bhqs', q1i, k1).astype(jnp.float32) * scale
        s2 = jnp.einsum('bqhd,bshd->bhqs', q2i, k2).astype(jnp.float32) * scale
        p = jax.nn.softmax(s1, axis=-1) - lam * jax.nn.softmax(s2, axis=-1)
        return jnp.einsum('bhqs,bshe->bqhe', p.astype(v.dtype), v)

    o = lax.map(block, (q1b, q2b))
    return o.swapaxes(0, 1).reshape(B, L, H, 2 * D)


def _layer(x, layer_idx, pre_g, post_g, w_in, w_out, qn_g, kn_g, rel_bias,
           lq1, lk1, lq2, lk2, subln_g):
    B, L, _ = x.shape
    h = _rmsnorm(x, pre_g)
    proj = jnp.einsum('bld,de->ble', h, w_in)
    qa, ka, va, qb, kb, vb, qc, kc, vc, gate = jnp.split(proj, SPLITS, axis=-1)
    t = jnp.arange(L)

    ang_row = _angles(t // GRID_W, HEAD_DIM // 2, AXIAL_THETA)
    ang_col = _angles(t % GRID_W, HEAD_DIM // 2, AXIAL_THETA)
    qa = _axial_rope(_rmsnorm(qa.reshape(B, L, A_HEADS, HEAD_DIM), qn_g), ang_row, ang_col)
    ka = _axial_rope(_rmsnorm(ka.reshape(B, L, A_KV_HEADS, HEAD_DIM), kn_g), ang_row, ang_col)
    va = va.reshape(B, L, A_KV_HEADS, HEAD_DIM)
    out_a = _gqa_attention(qa, ka, va)

    out_b = _neighborhood_attention(qb.reshape(B, L, B_HEADS, HEAD_DIM),
                                    kb.reshape(B, L, B_HEADS, HEAD_DIM),
                                    vb.reshape(B, L, B_HEADS, HEAD_DIM), rel_bias)

    ang_t = _angles(t, ROPE_DIMS, ROPE_THETA)
    qc = qc.reshape(B, L, C_HEADS, 2, HEAD_DIM)
    kc = kc.reshape(B, L, C_HEADS, 2, HEAD_DIM)
    q1 = _partial_rope(qc[:, :, :, 0], ang_t)
    q2 = _partial_rope(qc[:, :, :, 1], ang_t)
    k1 = _partial_rope(kc[:, :, :, 0], ang_t)
    k2 = _partial_rope(kc[:, :, :, 1], ang_t)
    lam_init = 0.8 - 0.6 * math.exp(-0.3 * layer_idx)
    lam = (jnp.exp(jnp.sum(lq1.astype(jnp.float32) * lk1.astype(jnp.float32)))
           - jnp.exp(jnp.sum(lq2.astype(jnp.float32) * lk2.astype(jnp.float32))) + lam_init)
    out_c = _diff_attention(q1, q2, k1, k2, vc.reshape(B, L, C_HEADS, 2 * HEAD_DIM), lam)
    out_c = (_rmsnorm(out_c, subln_g) * (1.0 - lam_init)).reshape(B, L, C_V)

    mix = jnp.concatenate([out_a, out_b, out_c], axis=-1) * jax.nn.silu(gate)
    y = jnp.einsum('ble,ed->bld', mix, w_out)
    return x + _rmsnorm(y, post_g)


def _trunk(x, pre_norm_g, post_norm_g, w_in, w_out, a_q_norm_g, a_k_norm_g, b_rel_bias,
           c_lambda_q1, c_lambda_k1, c_lambda_q2, c_lambda_k2, c_subln_g):
    for l in range(DEPTH):
        x = _layer(x, l, pre_norm_g[l], post_norm_g[l], w_in[l], w_out[l],
                   a_q_norm_g[l], a_k_norm_g[l], b_rel_bias[l],
                   c_lambda_q1[l], c_lambda_k1[l], c_lambda_q2[l], c_lambda_k2[l], c_subln_g[l])
    return x


def setup_inputs(seed: int = 0) -> dict:
    key = jax.random.key(seed)
    ks = jax.random.split(key, 14)
    f32 = jnp.float32
    nrm = jax.random.normal
    return {
        "x_prompt": nrm(ks[0], (BATCH, SEQ, D_MODEL), f32),
        "x_sample": nrm(ks[1], (DEC_BATCH, DEC_SEQ, D_MODEL), f32),
        "pre_norm_g": 1.0 + 0.02 * nrm(ks[2], (DEPTH, D_MODEL), f32),
        "post_norm_g": 1.0 + 0.02 * nrm(ks[3], (DEPTH, D_MODEL), f32),
        "w_in": nrm(ks[4], (DEPTH, D_MODEL, D_IN), f32) * D_MODEL ** -0.5,
        "w_out": nrm(ks[5], (DEPTH, D_MIX, D_MODEL), f32) * D_MIX ** -0.5,
        "a_q_norm_g": 1.0 + 0.02 * nrm(ks[6], (DEPTH, HEAD_DIM), f32),
        "a_k_norm_g": 1.0 + 0.02 * nrm(ks[7], (DEPTH, HEAD_DIM), f32),
        "b_rel_bias": 0.1 * nrm(ks[8], (DEPTH, B_HEADS, 2 * NA_ROWS - 1, 2 * NA_COLS - 1), f32),
        "c_lambda_q1": 0.1 * nrm(ks[9], (DEPTH, HEAD_DIM), f32),
        "c_lambda_k1": 0.1 * nrm(ks[10], (DEPTH, HEAD_DIM), f32),
        "c_lambda_q2": 0.1 * nrm(ks[11], (DEPTH, HEAD_DIM), f32),
        "c_lambda_k2": 0.1 * nrm(ks[12], (DEPTH, HEAD_DIM), f32),
        "c_subln_g": 1.0 + 0.02 * nrm(ks[13], (DEPTH, 2 * HEAD_DIM), f32),
    }


def reference(x_prompt, x_sample, pre_norm_g, post_norm_g, w_in, w_out, a_q_norm_g, a_k_norm_g,
              b_rel_bias, c_lambda_q1, c_lambda_k1, c_lambda_q2, c_lambda_k2, c_subln_g):
    y_prompt = _trunk(x_prompt, pre_norm_g, post_norm_g, w_in, w_out, a_q_norm_g, a_k_norm_g,
                      b_rel_bias, c_lambda_q1, c_lambda_k1, c_lambda_q2, c_lambda_k2, c_subln_g)
    y_sample = _trunk(x_sample, pre_norm_g, post_norm_g, w_in, w_out, a_q_norm_g, a_k_norm_g,
                      b_rel_bias, c_lambda_q1, c_lambda_k1, c_lambda_q2, c_lambda_k2, c_subln_g)
    return (y_prompt, y_sample)
```

```python
import functools
import math

import jax
import jax.numpy as jnp
from jax import lax
from jax.experimental import pallas as pl
from jax.experimental.pallas import tpu as pltpu

F32 = jnp.float32
BF16 = jnp.bfloat16

D_MODEL = 4096
HEAD_DIM = 128
GRID_W = 64
A_HEADS = 12
A_KV_HEADS = 4
A_GROUP = A_HEADS // A_KV_HEADS
B_HEADS = 8
C_HEADS = 6
A_Q = A_HEADS * HEAD_DIM
A_KV = A_KV_HEADS * HEAD_DIM
B_W = B_HEADS * HEAD_DIM
C_QK = C_HEADS * 2 * HEAD_DIM
C_V = C_HEADS * 2 * HEAD_DIM
D_MIX = A_Q + B_W + C_V
D_IN = A_Q + 2 * A_KV + 3 * B_W + 2 * C_QK + C_V + D_MIX
NA_ROWS = 8
NA_COLS = 16
AXIAL_THETA = 10000.0
ROPE_THETA = 500000.0
ROPE_DIMS = HEAD_DIM // 4
EPS = 1e-6
NEG_INF = -1e30
SCALE = 1.0 / math.sqrt(HEAD_DIM)

OFF_QA = 0
OFF_KA = OFF_QA + A_Q
OFF_VA = OFF_KA + A_KV
OFF_QB = OFF_VA + A_KV
OFF_KB = OFF_QB + B_W
OFF_VB = OFF_KB + B_W
OFF_QC = OFF_VB + B_W
OFF_KC = OFF_QC + C_QK
OFF_VC = OFF_KC + C_QK
OFF_GATE = OFF_VC + C_V

VMEM_LIMIT_BYTES = 56 * 1024 * 1024

IN_TM = 512
IN_TN = 512
OUT_TM = 256
OUT_TN = 512
A_TQ = 256
A_TK = 512
B_TQ = 1024
C_TQ = 512
C_TK = 512


def _nt_dot(a, b):
    return lax.dot_general(a, b, (((1,), (1,)), ((), ())), preferred_element_type=F32)


def _swap_halves(x, h):
    lane = lax.broadcasted_iota(jnp.int32, x.shape, 1)
    first = (lane % (2 * h)) < h
    return jnp.where(first, pltpu.roll(x, HEAD_DIM - h, 1), pltpu.roll(x, h, 1))


def _silu(g):
    return g / (1.0 + jnp.exp(-g))


def _in_proj_kernel(x_ref, g_ref, w_ref, qn_ref, kn_ref, ca_ref, sa_ref, cc_ref, sc_ref,
                    o_ref, h_scr):
    j = pl.program_id(1)

    @pl.when(j == 0)
    def _():
        x = x_ref[...]
        ms = jnp.sum(x * x, axis=-1, keepdims=True) * (1.0 / D_MODEL)
        h_scr[...] = ((x * lax.rsqrt(ms + EPS)) * g_ref[...]).astype(BF16)

    acc = jnp.dot(h_scr[...], w_ref[...], preferred_element_type=F32)
    heads = IN_TN // HEAD_DIM

    def head_norm(y, g):
        ms = jnp.sum(y * y, axis=-1, keepdims=True) * (1.0 / HEAD_DIM)
        return (y * lax.rsqrt(ms + EPS)) * g

    def axial(y):
        return y * ca_ref[...] + _swap_halves(y, HEAD_DIM // 4) * sa_ref[...]

    def partial(y):
        return y * cc_ref[...] + _swap_halves(y, ROPE_DIMS // 2) * sc_ref[...]

    def emit(fn):
        for hh in range(heads):
            sl = slice(hh * HEAD_DIM, (hh + 1) * HEAD_DIM)
            o_ref[:, sl] = fn(acc[:, sl]).astype(BF16)

    t_qa = OFF_KA // IN_TN
    t_ka = OFF_VA // IN_TN
    t_qb0, t_qb1 = OFF_QB // IN_TN, OFF_KB // IN_TN
    t_qc0, t_kc0, t_vc0 = OFF_QC // IN_TN, OFF_KC // IN_TN, OFF_VC // IN_TN

    is_qa = j < t_qa
    is_ka = jnp.logical_and(j >= t_qa, j < t_ka)
    is_qb = jnp.logical_and(j >= t_qb0, j < t_qb1)
    is_qc = jnp.logical_and(j >= t_qc0, j < t_kc0)
    is_kc = jnp.logical_and(j >= t_kc0, j < t_vc0)
    special = is_qa | is_ka | is_qb | is_qc | is_kc

    @pl.when(is_qa)
    def _():
        emit(lambda y: axial(head_norm(y, qn_ref[...])) * SCALE)

    @pl.when(is_ka)
    def _():
        emit(lambda y: axial(head_norm(y, kn_ref[...])))

    @pl.when(is_qb)
    def _():
        emit(lambda y: y * SCALE)

    @pl.when(is_qc)
    def _():
        emit(lambda y: partial(y) * SCALE)

    @pl.when(is_kc)
    def _():
        emit(partial)

    @pl.when(jnp.logical_not(special))
    def _():
        o_ref[...] = acc.astype(BF16)


def _in_proj(x2d, pre_g, w_in_bf, qn_g, kn_g, tables, seq_len):
    tokens = x2d.shape[0]
    tm = IN_TM
    assert tokens % tm == 0 and seq_len % tm == 0 and D_IN % IN_TN == 0
    for off in (OFF_KA, OFF_VA, OFF_QB, OFF_KB, OFF_QC, OFF_KC, OFF_VC):
        assert off % IN_TN == 0
    pos_blocks = seq_len // tm
    tab_spec = pl.BlockSpec((tm, HEAD_DIM), lambda i, j: (i % pos_blocks, 0))
    vec_spec = pl.BlockSpec((1, HEAD_DIM), lambda i, j: (0, 0))
    return pl.pallas_call(
        _in_proj_kernel,
        grid=(tokens // tm, D_IN // IN_TN),
        in_specs=[
            pl.BlockSpec((tm, D_MODEL), lambda i, j: (i, 0)),
            pl.BlockSpec((1, D_MODEL), lambda i, j: (0, 0)),
            pl.BlockSpec((D_MODEL, IN_TN), lambda i, j: (0, j)),
            vec_spec, vec_spec, tab_spec, tab_spec, tab_spec, tab_spec,
        ],
        out_specs=pl.BlockSpec((tm, IN_TN), lambda i, j: (i, j)),
        out_shape=jax.ShapeDtypeStruct((tokens, D_IN), BF16),
        scratch_shapes=[pltpu.VMEM((tm, D_MODEL), BF16)],
        compiler_params=pltpu.CompilerParams(
            dimension_semantics=("parallel", "arbitrary"),
            vmem_limit_bytes=VMEM_LIMIT_BYTES),
        name="in_proj",
    )(x2d, pre_g.reshape(1, D_MODEL), w_in_bf, qn_g.reshape(1, HEAD_DIM),
      kn_g.reshape(1, HEAD_DIM), *tables)


def _attn_a_kernel(q_ref, k_ref, v_ref, g0_ref, g1_ref, g2_ref, o_ref,
                   q_scr, m_scr, l_scr, acc_scr, *, tq, tk, n_kv):
    gate_refs = (g0_ref, g1_ref, g2_ref)
    for g in range(A_GROUP):
        q_scr[g * tq:(g + 1) * tq, :] = q_ref[:, g * HEAD_DIM:(g + 1) * HEAD_DIM]
    m_scr[...] = jnp.full(m_scr.shape, NEG_INF, F32)
    l_scr[...] = jnp.zeros(l_scr.shape, F32)
    acc_scr[...] = jnp.zeros(acc_scr.shape, F32)

    def body(j, carry):
        start = pl.multiple_of(j * tk, tk)
        k = k_ref[pl.ds(start, tk), :]
        v = v_ref[pl.ds(start, tk), :]
        s = _nt_dot(q_scr[...], k)
        m_prev = m_scr[...]
        m_new = jnp.maximum(m_prev, jnp.max(s, axis=-1, keepdims=True))
        p = jnp.exp(s - m_new)
        alpha = jnp.exp(m_prev - m_new)
        l_scr[...] = alpha * l_scr[...] + jnp.sum(p, axis=-1, keepdims=True)
        acc_scr[...] = alpha * acc_scr[...] + jnp.dot(
            p.astype(BF16), v, preferred_element_type=F32)
        m_scr[...] = m_new
        return carry

    lax.fori_loop(0, n_kv, body, 0)
    out = acc_scr[...] / l_scr[...]
    for g in range(A_GROUP):
        sl = slice(g * HEAD_DIM, (g + 1) * HEAD_DIM)
        gate = gate_refs[g][...].astype(F32)
        o_ref[:, sl] = (out[g * tq:(g + 1) * tq, :] * _silu(gate)).astype(BF16)


def _attn_a(proj, batch, seq_len):
    tq, tk = A_TQ, A_TK
    assert seq_len % tq == 0 and seq_len % tk == 0
    nq = seq_len // tq
    gw = A_GROUP * HEAD_DIM
    kernel = functools.partial(_attn_a_kernel, tq=tq, tk=tk, n_kv=seq_len // tk)

    def gate_spec(g):
        return pl.BlockSpec(
            (tq, HEAD_DIM),
            lambda b, h, i: (b * nq + i, OFF_GATE // HEAD_DIM + A_GROUP * h + g))

    return pl.pallas_call(
        kernel,
        grid=(batch, A_KV_HEADS, nq),
        in_specs=[
            pl.BlockSpec((tq, gw), lambda b, h, i: (b * nq + i, h)),
            pl.BlockSpec((seq_len, HEAD_DIM), lambda b, h, i: (b, OFF_KA // HEAD_DIM + h)),
            pl.BlockSpec((seq_len, HEAD_DIM), lambda b, h, i: (b, OFF_VA // HEAD_DIM + h)),
            gate_spec(0), gate_spec(1), gate_spec(2),
        ],
        out_specs=pl.BlockSpec((tq, gw), lambda b, h, i: (b * nq + i, h)),
        out_shape=jax.ShapeDtypeStruct((batch * seq_len, A_Q), BF16),
        scratch_shapes=[
            pltpu.VMEM((A_GROUP * tq, HEAD_DIM), BF16),
            pltpu.VMEM((A_GROUP * tq, 1), F32),
            pltpu.VMEM((A_GROUP * tq, 1), F32),
            pltpu.VMEM((A_GROUP * tq, HEAD_DIM), F32),
        ],
        compiler_params=pltpu.CompilerParams(
            dimension_semantics=("parallel", "parallel", "arbitrary"),
            vmem_limit_bytes=VMEM_LIMIT_BYTES),
        name="attn_a",
    )(proj, proj, proj, proj, proj, proj)


def _attn_b_kernel(q_ref, k_ref, v_ref, gate_ref, bias_ref, o_ref, *, rows_per_step, grid_rows):
    t = pl.program_id(2)
    band = NA_ROWS * GRID_W

    def body(i, carry):
        r = t * rows_per_step + i
        start_row = jnp.clip(r - NA_ROWS // 2, 0, grid_rows - NA_ROWS)
        d0 = start_row - r + (NA_ROWS - 1)
        q_off = pl.multiple_of(i * GRID_W, GRID_W)
        k_off = pl.multiple_of(start_row * GRID_W, GRID_W)
        q = q_ref[pl.ds(q_off, GRID_W), :]
        k = k_ref[pl.ds(k_off, band), :]
        v = v_ref[pl.ds(k_off, band), :]
        s = _nt_dot(q, k) + bias_ref[d0]
        m = jnp.max(s, axis=-1, keepdims=True)
        p = jnp.exp(s - m)
        l = jnp.sum(p, axis=-1, keepdims=True)
        o = jnp.dot(p.astype(BF16), v, preferred_element_type=F32) / l
        gate = gate_ref[pl.ds(q_off, GRID_W), :].astype(F32)
        o_ref[pl.ds(q_off, GRID_W), :] = (o * _silu(gate)).astype(BF16)
        return carry

    lax.fori_loop(0, rows_per_step, body, 0)


def _attn_b(proj, bias_bands, batch, seq_len):
    tq = min(B_TQ, seq_len)
    assert seq_len % tq == 0 and tq % GRID_W == 0
    grid_rows = seq_len // GRID_W
    assert grid_rows >= NA_ROWS
    nq = seq_len // tq
    kernel = functools.partial(_attn_b_kernel, rows_per_step=tq // GRID_W, grid_rows=grid_rows)
    return pl.pallas_call(
        kernel,
        grid=(batch, B_HEADS, nq),
        in_specs=[
            pl.BlockSpec((tq, HEAD_DIM), lambda b, h, i: (b * nq + i, OFF_QB // HEAD_DIM + h)),
            pl.BlockSpec((seq_len, HEAD_DIM), lambda b, h, i: (b, OFF_KB // HEAD_DIM + h)),
            pl.BlockSpec((seq_len, HEAD_DIM), lambda b, h, i: (b, OFF_VB // HEAD_DIM + h)),
            pl.BlockSpec((tq, HEAD_DIM),
                         lambda b, h, i: (b * nq + i, (OFF_GATE + A_Q) // HEAD_DIM + h)),
            pl.BlockSpec((None, NA_ROWS, GRID_W, NA_ROWS * GRID_W), lambda b, h, i: (h, 0, 0, 0)),
        ],
        out_specs=pl.BlockSpec((tq, HEAD_DIM), lambda b, h, i: (b * nq + i, h)),
        out_shape=jax.ShapeDtypeStruct((batch * seq_len, B_W), BF16),
        compiler_params=pltpu.CompilerParams(
            dimension_semantics=("parallel", "parallel", "arbitrary"),
            vmem_limit_bytes=VMEM_LIMIT_BYTES),
        name="attn_b",
    )(proj, proj, proj, proj, bias_bands)


def _bias_bands(rel_bias):
    c = jnp.arange(GRID_W)
    col_start = jnp.clip(c - NA_COLS // 2, 0, GRID_W - NA_COLS)
    col_ok = (c[None, :] >= col_start[:, None]) & (c[None, :] < col_start[:, None] + NA_COLS)
    dc = jnp.clip(c[None, :] - c[:, None], -(NA_COLS - 1), NA_COLS - 1) + (NA_COLS - 1)
    expanded = jnp.where(col_ok[None, None], rel_bias[:, :, dc].astype(F32), NEG_INF)
    bands = [jnp.concatenate([expanded[:, d0 + w] for w in range(NA_ROWS)], axis=-1)
             for d0 in range(NA_ROWS)]
    return jnp.stack(bands, axis=1)


def _attn_c_kernel(q_ref, k_ref, v_ref, gate_ref, lq1_ref, lk1_ref, lq2_ref, lk2_ref, sg_ref,
                   o_ref, m1_scr, l1_scr, a1_scr, m2_scr, l2_scr, a2_scr,
                   *, tk, n_kv, lam_init):
    q1 = q_ref[:, :HEAD_DIM]
    q2 = q_ref[:, HEAD_DIM:]
    for m_scr, l_scr, a_scr in ((m1_scr, l1_scr, a1_scr), (m2_scr, l2_scr, a2_scr)):
        m_scr[...] = jnp.full(m_scr.shape, NEG_INF, F32)
        l_scr[...] = jnp.zeros(l_scr.shape, F32)
        a_scr[...] = jnp.zeros(a_scr.shape, F32)

    def update(q, k, v, m_scr, l_scr, a_scr):
        s = _nt_dot(q, k)
        m_prev = m_scr[...]
        m_new = jnp.maximum(m_prev, jnp.max(s, axis=-1, keepdims=True))
        p = jnp.exp(s - m_new)
        alpha = jnp.exp(m_prev - m_new)
        l_scr[...] = alpha * l_scr[...] + jnp.sum(p, axis=-1, keepdims=True)
        a_scr[...] = alpha * a_scr[...] + jnp.dot(p.astype(BF16), v, preferred_element_type=F32)
        m_scr[...] = m_new

    def body(j, carry):
        start = pl.multiple_of(j * tk, tk)
        k = k_ref[pl.ds(start, tk), :]
        v = v_ref[pl.ds(start, tk), :]
        update(q1, k[:, :HEAD_DIM], v, m1_scr, l1_scr, a1_scr)
        update(q2, k[:, HEAD_DIM:], v, m2_scr, l2_scr, a2_scr)
        return carry

    lax.fori_loop(0, n_kv, body, 0)

    lam = (jnp.exp(jnp.sum(lq1_ref[...] * lk1_ref[...], axis=-1, keepdims=True))
           - jnp.exp(jnp.sum(lq2_ref[...] * lk2_ref[...], axis=-1, keepdims=True))
           + lam_init)
    o = a1_scr[...] / l1_scr[...] - lam * (a2_scr[...] / l2_scr[...])
    ms = jnp.sum(o * o, axis=-1, keepdims=True) * (1.0 / (2 * HEAD_DIM))
    o = (o * lax.rsqrt(ms + EPS)) * sg_ref[...] * (1.0 - lam_init)
    o_ref[...] = (o * _silu(gate_ref[...].astype(F32))).astype(BF16)


def _attn_c(proj, lq1, lk1, lq2, lk2, subln_g, lam_init, batch, seq_len):
    tq, tk = C_TQ, C_TK
    assert seq_len % tq == 0 and seq_len % tk == 0
    nq = seq_len // tq
    hw = 2 * HEAD_DIM
    kernel = functools.partial(_attn_c_kernel, tk=tk, n_kv=seq_len // tk, lam_init=lam_init)
    vec = pl.BlockSpec((1, HEAD_DIM), lambda b, h, i: (0, 0))
    return pl.pallas_call(
        kernel,
        grid=(batch, C_HEADS, nq),
        in_specs=[
            pl.BlockSpec((tq, hw), lambda b, h, i: (b * nq + i, OFF_QC // hw + h)),
            pl.BlockSpec((seq_len, hw), lambda b, h, i: (b, OFF_KC // hw + h)),
            pl.BlockSpec((seq_len, hw), lambda b, h, i: (b, OFF_VC // hw + h)),
            pl.BlockSpec((tq, hw), lambda b, h, i: (b * nq + i, (OFF_GATE + A_Q + B_W) // hw + h)),
            vec, vec, vec, vec,
            pl.BlockSpec((1, hw), lambda b, h, i: (0, 0)),
        ],
        out_specs=pl.BlockSpec((tq, hw), lambda b, h, i: (b * nq + i, h)),
        out_shape=jax.ShapeDtypeStruct((batch * seq_len, C_V), BF16),
        scratch_shapes=[
            pltpu.VMEM((tq, 1), F32), pltpu.VMEM((tq, 1), F32), pltpu.VMEM((tq, hw), F32),
            pltpu.VMEM((tq, 1), F32), pltpu.VMEM((tq, 1), F32), pltpu.VMEM((tq, hw), F32),
        ],
        compiler_params=pltpu.CompilerParams(
            dimension_semantics=("parallel", "parallel", "arbitrary"),
            vmem_limit_bytes=VMEM_LIMIT_BYTES),
        name="attn_c",
    )(proj, proj, proj, proj, lq1.reshape(1, HEAD_DIM), lk1.reshape(1, HEAD_DIM),
      lq2.reshape(1, HEAD_DIM), lk2.reshape(1, HEAD_DIM), subln_g.reshape(1, hw))


def _out_proj_kernel(a_ref, b_ref, c_ref, wa_ref, wb_ref, wc_ref, x_ref, g_ref, o_ref, y_scr,
                     *, n_col_tiles):
    j = pl.program_id(1)
    y = jnp.dot(a_ref[...], wa_ref[...], preferred_element_type=F32)
    y = y + jnp.dot(b_ref[...], wb_ref[...], preferred_element_type=F32)
    y = y + jnp.dot(c_ref[...], wc_ref[...], preferred_element_type=F32)
    y_scr[j] = y

    @pl.when(j == n_col_tiles - 1)
    def _():
        ss = jnp.zeros((y_scr.shape[1], 1), F32)
        for jj in range(n_col_tiles):
            t = y_scr[jj]
            ss = ss + jnp.sum(t * t, axis=-1, keepdims=True)
        inv = lax.rsqrt(ss * (1.0 / D_MODEL) + EPS)
        for jj in range(n_col_tiles):
            sl = slice(jj * OUT_TN, (jj + 1) * OUT_TN)
            o_ref[:, sl] = x_ref[:, sl] + (y_scr[jj] * inv) * g_ref[:, sl]


def _out_proj(mix_a, mix_b, mix_c, wa, wb, wc, x2d, post_g):
    tokens = x2d.shape[0]
    tm, tn = OUT_TM, OUT_TN
    assert tokens % tm == 0 and D_MODEL % tn == 0
    n_col_tiles = D_MODEL // tn
    kernel = functools.partial(_out_proj_kernel, n_col_tiles=n_col_tiles)
    return pl.pallas_call(
        kernel,
        grid=(tokens // tm, n_col_tiles),
        in_specs=[
            pl.BlockSpec((tm, A_Q), lambda i, j: (i, 0)),
            pl.BlockSpec((tm, B_W), lambda i, j: (i, 0)),
            pl.BlockSpec((tm, C_V), lambda i, j: (i, 0)),
            pl.BlockSpec((A_Q, tn), lambda i, j: (0, j)),
            pl.BlockSpec((B_W, tn), lambda i, j: (0, j)),
            pl.BlockSpec((C_V, tn), lambda i, j: (0, j)),
            pl.BlockSpec((tm, D_MODEL), lambda i, j: (i, 0)),
            pl.BlockSpec((1, D_MODEL), lambda i, j: (0, 0)),
        ],
        out_specs=pl.BlockSpec((tm, D_MODEL), lambda i, j: (i, 0)),
        out_shape=jax.ShapeDtypeStruct((tokens, D_MODEL), F32),
        scratch_shapes=[pltpu.VMEM((n_col_tiles, tm, tn), F32)],
        compiler_params=pltpu.CompilerParams(
            dimension_semantics=("parallel", "arbitrary"),
            vmem_limit_bytes=VMEM_LIMIT_BYTES),
        name="out_proj",
    )(mix_a, mix_b, mix_c, wa, wb, wc, x2d, post_g.reshape(1, D_MODEL))


def _angles(pos, dim, theta):
    inv = jnp.power(theta, -jnp.arange(0, dim, 2, dtype=F32) / dim)
    return pos.astype(F32)[:, None] * inv[None, :]


def _rope_tables(seq_len):
    t = jnp.arange(seq_len)
    ang_row = _angles(t // GRID_W, HEAD_DIM // 2, AXIAL_THETA)
    ang_col = _angles(t % GRID_W, HEAD_DIM // 2, AXIAL_THETA)
    cr, sr, cc_, sc_ = jnp.cos(ang_row), jnp.sin(ang_row), jnp.cos(ang_col), jnp.sin(ang_col)
    cos_ax = jnp.concatenate([cr, cr, cc_, cc_], axis=-1)
    sin_ax = jnp.concatenate([-sr, sr, -sc_, sc_], axis=-1)
    ang_t = _angles(t, ROPE_DIMS, ROPE_THETA)
    ct, st = jnp.cos(ang_t), jnp.sin(ang_t)
    rest = HEAD_DIM - ROPE_DIMS
    cos_p = jnp.concatenate([ct, ct, jnp.ones((seq_len, rest), F32)], axis=-1)
    sin_p = jnp.concatenate([-st, st, jnp.zeros((seq_len, rest), F32)], axis=-1)
    return cos_ax, sin_ax, cos_p, sin_p


def _trunk(x, params, w_in_bf, w_out_parts, bias_bands):
    batch, seq_len, _ = x.shape
    tables = _rope_tables(seq_len)
    x2d = x.reshape(batch * seq_len, D_MODEL)
    depth = w_in_bf.shape[0]
    for l in range(depth):
        lam_init = 0.8 - 0.6 * math.exp(-0.3 * l)
        proj = _in_proj(x2d, params["pre_norm_g"][l], w_in_bf[l], params["a_q_norm_g"][l],
                        params["a_k_norm_g"][l], tables, seq_len)
        mix_a = _attn_a(proj, batch, seq_len)
        mix_b = _attn_b(proj, bias_bands[l], batch, seq_len)
        mix_c = _attn_c(proj, params["c_lambda_q1"][l], params["c_lambda_k1"][l],
                        params["c_lambda_q2"][l], params["c_lambda_k2"][l],
                        params["c_subln_g"][l], lam_init, batch, seq_len)
        wa, wb, wc = w_out_parts[l]
        x2d = _out_proj(mix_a, mix_b, mix_c, wa, wb, wc, x2d, params["post_norm_g"][l])
    return x2d.reshape(batch, seq_len, D_MODEL)


def kernel(x_prompt, x_sample, pre_norm_g, post_norm_g, w_in, w_out, a_q_norm_g, a_k_norm_g,
           b_rel_bias, c_lambda_q1, c_lambda_k1, c_lambda_q2, c_lambda_k2, c_subln_g):
    params = dict(pre_norm_g=pre_norm_g, post_norm_g=post_norm_g, a_q_norm_g=a_q_norm_g,
                  a_k_norm_g=a_k_norm_g, c_lambda_q1=c_lambda_q1, c_lambda_k1=c_lambda_k1,
                  c_lambda_q2=c_lambda_q2, c_lambda_k2=c_lambda_k2, c_subln_g=c_subln_g)
    depth = w_in.shape[0]
    w_in_bf = w_in.astype(BF16)
    w_out_bf = w_out.astype(BF16)
    w_out_parts = [(w_out_bf[l, :A_Q], w_out_bf[l, A_Q:A_Q + B_W], w_out_bf[l, A_Q + B_W:])
                   for l in range(depth)]
    bias_bands = [_bias_bands(b_rel_bias[l]) for l in range(depth)]
    y_prompt = _trunk(x_prompt, params, w_in_bf, w_out_parts, bias_bands)
    y_sample = _trunk(x_sample, params, w_in_bf, w_out_parts, bias_bands)
    return (y_prompt, y_sample)
```

```python
import functools
import math

import jax
import jax.numpy as jnp
from jax import lax
from jax.experimental import pallas as pl
from jax.experimental.pallas import tpu as pltpu

F32 = jnp.float32
BF16 = jnp.bfloat16

D_MODEL = 4096
HEAD_DIM = 128
GRID_W = 64
A_HEADS = 12
A_KV_HEADS = 4
A_GROUP = A_HEADS // A_KV_HEADS
B_HEADS = 8
C_HEADS = 6
A_Q = A_HEADS * HEAD_DIM
A_KV = A_KV_HEADS * HEAD_DIM
B_W = B_HEADS * HEAD_DIM
C_QK = C_HEADS * 2 * HEAD_DIM
C_V = C_HEADS * 2 * HEAD_DIM
D_MIX = A_Q + B_W + C_V
D_IN = A_Q + 2 * A_KV + 3 * B_W + 2 * C_QK + C_V + D_MIX
NA_ROWS = 8
NA_COLS = 16
AXIAL_THETA = 10000.0
ROPE_THETA = 500000.0
ROPE_DIMS = HEAD_DIM // 4
EPS = 1e-6
NEG_INF = -1e30
SCALE = 1.0 / math.sqrt(HEAD_DIM)
SCALE_LOG2 = SCALE * math.log2(math.e)
ONES_ROWS = 16

OFF_QA = 0
OFF_KA = OFF_QA + A_Q
OFF_VA = OFF_KA + A_KV
OFF_QB = OFF_VA + A_KV
OFF_KB = OFF_QB + B_W
OFF_VB = OFF_KB + B_W
OFF_QC = OFF_VB + B_W
OFF_KC = OFF_QC + C_QK
OFF_VC = OFF_KC + C_QK
OFF_GATE = OFF_VC + C_V

VMEM_LIMIT_BYTES = 56 * 1024 * 1024

IN_TM = 512
IN_TN = 512
OUT_TM = 256
OUT_TN = 512
A_TQ = 256
A_TK = 512
B_TQ = 1024
C_TQ = 512
C_TK = 512


def _nt_dot(a, b):
    return lax.dot_general(a, b, (((1,), (1,)), ((), ())), preferred_element_type=F32)


def _swap_halves(x, h):
    lane = lax.broadcasted_iota(jnp.int32, x.shape, 1)
    first = (lane % (2 * h)) < h
    return jnp.where(first, pltpu.roll(x, HEAD_DIM - h, 1), pltpu.roll(x, h, 1))


def _silu(g):
    return g / (1.0 + jnp.exp(-g))


def _in_proj_kernel(x_ref, g_ref, w_ref, qn_ref, kn_ref, ca_ref, sa_ref, cc_ref, sc_ref,
                    o_ref, h_scr):
    j = pl.program_id(1)

    @pl.when(j == 0)
    def _():
        x = x_ref[...]
        ms = jnp.sum(x * x, axis=-1, keepdims=True) * (1.0 / D_MODEL)
        h_scr[...] = ((x * lax.rsqrt(ms + EPS)) * g_ref[...]).astype(BF16)

    acc = jnp.dot(h_scr[...], w_ref[...], preferred_element_type=F32)
    heads = IN_TN // HEAD_DIM

    def head_norm(y, g):
        ms = jnp.sum(y * y, axis=-1, keepdims=True) * (1.0 / HEAD_DIM)
        return (y * lax.rsqrt(ms + EPS)) * g

    def axial(y):
        return y * ca_ref[...] + _swap_halves(y, HEAD_DIM // 4) * sa_ref[...]

    def partial(y):
        return y * cc_ref[...] + _swap_halves(y, ROPE_DIMS // 2) * sc_ref[...]

    def emit(fn):
        for hh in range(heads):
            sl = slice(hh * HEAD_DIM, (hh + 1) * HEAD_DIM)
            o_ref[:, sl] = fn(acc[:, sl]).astype(BF16)

    t_qa = OFF_KA // IN_TN
    t_ka = OFF_VA // IN_TN
    t_qb0, t_qb1 = OFF_QB // IN_TN, OFF_KB // IN_TN
    t_qc0, t_kc0, t_vc0 = OFF_QC // IN_TN, OFF_KC // IN_TN, OFF_VC // IN_TN

    is_qa = j < t_qa
    is_ka = jnp.logical_and(j >= t_qa, j < t_ka)
    is_qb = jnp.logical_and(j >= t_qb0, j < t_qb1)
    is_qc = jnp.logical_and(j >= t_qc0, j < t_kc0)
    is_kc = jnp.logical_and(j >= t_kc0, j < t_vc0)
    special = is_qa | is_ka | is_qb | is_qc | is_kc

    @pl.when(is_qa)
    def _():
        emit(lambda y: axial(head_norm(y, qn_ref[...])) * SCALE_LOG2)

    @pl.when(is_ka)
    def _():
        emit(lambda y: axial(head_norm(y, kn_ref[...])))

    @pl.when(is_qb)
    def _():
        emit(lambda y: y * SCALE)

    @pl.when(is_qc)
    def _():
        emit(lambda y: partial(y) * SCALE_LOG2)

    @pl.when(is_kc)
    def _():
        emit(partial)

    @pl.when(jnp.logical_not(special))
    def _():
        o_ref[...] = acc.astype(BF16)


def _in_proj(x2d, pre_g, w_in_bf, qn_g, kn_g, tables, seq_len):
    tokens = x2d.shape[0]
    tm = IN_TM
    assert tokens % tm == 0 and seq_len % tm == 0 and D_IN % IN_TN == 0
    for off in (OFF_KA, OFF_VA, OFF_QB, OFF_KB, OFF_QC, OFF_KC, OFF_VC):
        assert off % IN_TN == 0
    pos_blocks = seq_len // tm
    tab_spec = pl.BlockSpec((tm, HEAD_DIM), lambda i, j: (i % pos_blocks, 0))
    vec_spec = pl.BlockSpec((1, HEAD_DIM), lambda i, j: (0, 0))
    return pl.pallas_call(
        _in_proj_kernel,
        grid=(tokens // tm, D_IN // IN_TN),
        in_specs=[
            pl.BlockSpec((tm, D_MODEL), lambda i, j: (i, 0)),
            pl.BlockSpec((1, D_MODEL), lambda i, j: (0, 0)),
            pl.BlockSpec((D_MODEL, IN_TN), lambda i, j: (0, j)),
            vec_spec, vec_spec, tab_spec, tab_spec, tab_spec, tab_spec,
        ],
        out_specs=pl.BlockSpec((tm, IN_TN), lambda i, j: (i, j)),
        out_shape=jax.ShapeDtypeStruct((tokens, D_IN), BF16),
        scratch_shapes=[pltpu.VMEM((tm, D_MODEL), BF16)],
        compiler_params=pltpu.CompilerParams(
            dimension_semantics=("parallel", "arbitrary"),
            vmem_limit_bytes=VMEM_LIMIT_BYTES),
        name="in_proj",
    )(x2d, pre_g.reshape(1, D_MODEL), w_in_bf, qn_g.reshape(1, HEAD_DIM),
      kn_g.reshape(1, HEAD_DIM), *tables)


def _build_vt(v_ref, vt_scr, *, n_kv, tk, dv):
    def chunk(c, carry):
        off = pl.multiple_of(c * tk, tk)
        vt_scr[c, 0:dv, :] = v_ref[pl.ds(off, tk), :].astype(F32).T.astype(BF16)
        row = lax.broadcasted_iota(jnp.int32, (ONES_ROWS, tk), 0)
        vt_scr[c, dv:dv + ONES_ROWS, :] = jnp.where(row == 0, 1.0, 0.0).astype(BF16)
        return carry

    lax.fori_loop(0, n_kv, chunk, 0)


def _flash_step(s, vt, m_scr, acc_scr):
    m_old = m_scr[...]
    m_new = jnp.maximum(m_old, jnp.max(s, axis=0, keepdims=True))
    alpha = jnp.exp2(m_old - m_new)
    p = jnp.exp2(s - m_new).astype(BF16)
    acc_scr[...] = alpha * acc_scr[...] + jnp.dot(vt, p, preferred_element_type=F32)
    m_scr[...] = m_new


def _attn_a_kernel(q_ref, k_ref, v_ref, g0_ref, g1_ref, g2_ref, o_ref,
                   qt_scr, vt_scr, s_scr, m_scr, acc_scr, *, tq, tk, n_kv):
    gate_refs = (g0_ref, g1_ref, g2_ref)

    @pl.when(pl.program_id(2) == 0)
    def _():
        _build_vt(v_ref, vt_scr, n_kv=n_kv, tk=tk, dv=HEAD_DIM)

    for g in range(A_GROUP):
        qt_scr[:, g * tq:(g + 1) * tq] = (
            q_ref[:, g * HEAD_DIM:(g + 1) * HEAD_DIM].astype(F32).T.astype(BF16))
    m_scr[...] = jnp.full(m_scr.shape, NEG_INF, F32)
    acc_scr[...] = jnp.zeros(acc_scr.shape, F32)

    def scores(j):
        start = pl.multiple_of(j * tk, tk)
        return jnp.dot(k_ref[pl.ds(start, tk), :], qt_scr[...], preferred_element_type=F32)

    s_scr[0] = scores(0)

    def body(jj, carry):
        for slot in range(2):
            j = 2 * jj + slot
            s_scr[1 - slot] = scores(jnp.minimum(j + 1, n_kv - 1))
            _flash_step(s_scr[slot], vt_scr[j], m_scr, acc_scr)
        return carry

    lax.fori_loop(0, n_kv // 2, body, 0)
    acc = acc_scr[...]
    out = (acc[0:HEAD_DIM] / acc[HEAD_DIM:HEAD_DIM + 1]).T
    for g in range(A_GROUP):
        sl = slice(g * HEAD_DIM, (g + 1) * HEAD_DIM)
        gate = gate_refs[g][...].astype(F32)
        o_ref[:, sl] = (out[g * tq:(g + 1) * tq, :] * _silu(gate)).astype(BF16)


def _attn_a(proj, batch, seq_len):
    tq, tk = A_TQ, A_TK
    assert seq_len % tq == 0 and seq_len % (2 * tk) == 0
    nq = seq_len // tq
    n_kv = seq_len // tk
    gw = A_GROUP * HEAD_DIM
    kernel = functools.partial(_attn_a_kernel, tq=tq, tk=tk, n_kv=n_kv)

    def gate_spec(g):
        return pl.BlockSpec(
            (tq, HEAD_DIM),
            lambda b, h, i: (b * nq + i, OFF_GATE // HEAD_DIM + A_GROUP * h + g))

    return pl.pallas_call(
        kernel,
        grid=(batch, A_KV_HEADS, nq),
        in_specs=[
            pl.BlockSpec((tq, gw), lambda b, h, i: (b * nq + i, h)),
            pl.BlockSpec((seq_len, HEAD_DIM), lambda b, h, i: (b, OFF_KA // HEAD_DIM + h)),
            pl.BlockSpec((seq_len, HEAD_DIM), lambda b, h, i: (b, OFF_VA // HEAD_DIM + h)),
            gate_spec(0), gate_spec(1), gate_spec(2),
        ],
        out_specs=pl.BlockSpec((tq, gw), lambda b, h, i: (b * nq + i, h)),
        out_shape=jax.ShapeDtypeStruct((batch * seq_len, A_Q), BF16),
        scratch_shapes=[
            pltpu.VMEM((HEAD_DIM, A_GROUP * tq), BF16),
            pltpu.VMEM((n_kv, HEAD_DIM + ONES_ROWS, tk), BF16),
            pltpu.VMEM((2, tk, A_GROUP * tq), F32),
            pltpu.VMEM((1, A_GROUP * tq), F32),
            pltpu.VMEM((HEAD_DIM + ONES_ROWS, A_GROUP * tq), F32),
        ],
        compiler_params=pltpu.CompilerParams(
            dimension_semantics=("parallel", "parallel", "arbitrary"),
            vmem_limit_bytes=VMEM_LIMIT_BYTES),
        name="attn_a",
    )(proj, proj, proj, proj, proj, proj)


def _attn_b_kernel(q_ref, k_ref, v_ref, gate_ref, bias_ref, o_ref, *, rows_per_step, grid_rows):
    t = pl.program_id(2)
    band = NA_ROWS * GRID_W

    def body(i, carry):
        r = t * rows_per_step + i
        start_row = jnp.clip(r - NA_ROWS // 2, 0, grid_rows - NA_ROWS)
        d0 = start_row - r + (NA_ROWS - 1)
        q_off = pl.multiple_of(i * GRID_W, GRID_W)
        k_off = pl.multiple_of(start_row * GRID_W, GRID_W)
        q = q_ref[pl.ds(q_off, GRID_W), :]
        k = k_ref[pl.ds(k_off, band), :]
        v = v_ref[pl.ds(k_off, band), :]
        s = _nt_dot(q, k) + bias_ref[d0]
        m = jnp.max(s, axis=-1, keepdims=True)
        p = jnp.exp(s - m)
        l = jnp.sum(p, axis=-1, keepdims=True)
        o = jnp.dot(p.astype(BF16), v, preferred_element_type=F32) / l
        gate = gate_ref[pl.ds(q_off, GRID_W), :].astype(F32)
        o_ref[pl.ds(q_off, GRID_W), :] = (o * _silu(gate)).astype(BF16)
        return carry

    lax.fori_loop(0, rows_per_step, body, 0)


def _attn_b(proj, bias_bands, batch, seq_len):
    tq = min(B_TQ, seq_len)
    assert seq_len % tq == 0 and tq % GRID_W == 0
    grid_rows = seq_len // GRID_W
    assert grid_rows >= NA_ROWS
    nq = seq_len // tq
    kernel = functools.partial(_attn_b_kernel, rows_per_step=tq // GRID_W, grid_rows=grid_rows)
    return pl.pallas_call(
        kernel,
        grid=(batch, B_HEADS, nq),
        in_specs=[
            pl.BlockSpec((tq, HEAD_DIM), lambda b, h, i: (b * nq + i, OFF_QB // HEAD_DIM + h)),
            pl.BlockSpec((seq_len, HEAD_DIM), lambda b, h, i: (b, OFF_KB // HEAD_DIM + h)),
            pl.BlockSpec((seq_len, HEAD_DIM), lambda b, h, i: (b, OFF_VB // HEAD_DIM + h)),
            pl.BlockSpec((tq, HEAD_DIM),
                         lambda b, h, i: (b * nq + i, (OFF_GATE + A_Q) // HEAD_DIM + h)),
            pl.BlockSpec((None, NA_ROWS, GRID_W, NA_ROWS * GRID_W), lambda b, h, i: (h, 0, 0, 0)),
        ],
        out_specs=pl.BlockSpec((tq, HEAD_DIM), lambda b, h, i: (b * nq + i, h)),
        out_shape=jax.ShapeDtypeStruct((batch * seq_len, B_W), BF16),
        compiler_params=pltpu.CompilerParams(
            dimension_semantics=("parallel", "parallel", "arbitrary"),
            vmem_limit_bytes=VMEM_LIMIT_BYTES),
        name="attn_b",
    )(proj, proj, proj, proj, bias_bands)


def _bias_bands(rel_bias):
    c = jnp.arange(GRID_W)
    col_start = jnp.clip(c - NA_COLS // 2, 0, GRID_W - NA_COLS)
    col_ok = (c[None, :] >= col_start[:, None]) & (c[None, :] < col_start[:, None] + NA_COLS)
    dc = jnp.clip(c[None, :] - c[:, None], -(NA_COLS - 1), NA_COLS - 1) + (NA_COLS - 1)
    expanded = jnp.where(col_ok[None, None], rel_bias[:, :, dc].astype(F32), NEG_INF)
    bands = [jnp.concatenate([expanded[:, d0 + w] for w in range(NA_ROWS)], axis=-1)
             for d0 in range(NA_ROWS)]
    return jnp.stack(bands, axis=1)


def _attn_c_kernel(q_ref, k_ref, v_ref, gate_ref, lq1_ref, lk1_ref, lq2_ref, lk2_ref, sg_ref,
                   o_ref, qt_scr, vt_scr, s_scr, m1_scr, a1_scr, m2_scr, a2_scr,
                   *, tk, n_kv, lam_init):
    dv = 2 * HEAD_DIM
    streams = ((m1_scr, a1_scr), (m2_scr, a2_scr))

    @pl.when(pl.program_id(2) == 0)
    def _():
        _build_vt(v_ref, vt_scr, n_kv=n_kv, tk=tk, dv=dv)

    for u, (m_scr, a_scr) in enumerate(streams):
        qt_scr[u] = q_ref[:, u * HEAD_DIM:(u + 1) * HEAD_DIM].astype(F32).T.astype(BF16)
        m_scr[...] = jnp.full(m_scr.shape, NEG_INF, F32)
        a_scr[...] = jnp.zeros(a_scr.shape, F32)

    def scores(j, u):
        start = pl.multiple_of(j * tk, tk)
        k = k_ref[pl.ds(start, tk), u * HEAD_DIM:(u + 1) * HEAD_DIM]
        return jnp.dot(k, qt_scr[u], preferred_element_type=F32)

    for u in range(2):
        s_scr[0, u] = scores(0, u)

    def body(jj, carry):
        for slot in range(2):
            j = 2 * jj + slot
            nxt = jnp.minimum(j + 1, n_kv - 1)
            for u, (m_scr, a_scr) in enumerate(streams):
                s_scr[1 - slot, u] = scores(nxt, u)
                _flash_step(s_scr[slot, u], vt_scr[j], m_scr, a_scr)
        return carry

    lax.fori_loop(0, n_kv // 2, body, 0)

    lam = (jnp.exp(jnp.sum(lq1_ref[...] * lk1_ref[...], axis=-1, keepdims=True))
           - jnp.exp(jnp.sum(lq2_ref[...] * lk2_ref[...], axis=-1, keepdims=True))
           + lam_init)
    a1 = a1_scr[...]
    a2 = a2_scr[...]
    o = (a1[0:dv] / a1[dv:dv + 1] - lam * (a2[0:dv] / a2[dv:dv + 1])).T
    ms = jnp.sum(o * o, axis=-1, keepdims=True) * (1.0 / (2 * HEAD_DIM))
    o = (o * lax.rsqrt(ms + EPS)) * sg_ref[...] * (1.0 - lam_init)
    o_ref[...] = (o * _silu(gate_ref[...].astype(F32))).astype(BF16)


def _attn_c(proj, lq1, lk1, lq2, lk2, subln_g, lam_init, batch, seq_len):
    tq, tk = C_TQ, C_TK
    assert seq_len % tq == 0 and seq_len % (2 * tk) == 0
    nq = seq_len // tq
    n_kv = seq_len // tk
    hw = 2 * HEAD_DIM
    kernel = functools.partial(_attn_c_kernel, tk=tk, n_kv=n_kv, lam_init=lam_init)
    vec = pl.BlockSpec((1, HEAD_DIM), lambda b, h, i: (0, 0))
    return pl.pallas_call(
        kernel,
        grid=(batch, C_HEADS, nq),
        in_specs=[
            pl.BlockSpec((tq, hw), lambda b, h, i: (b * nq + i, OFF_QC // hw + h)),
            pl.BlockSpec((seq_len, hw), lambda b, h, i: (b, OFF_KC // hw + h)),
            pl.BlockSpec((seq_len, hw), lambda b, h, i: (b, OFF_VC // hw + h)),
            pl.BlockSpec((tq, hw), lambda b, h, i: (b * nq + i, (OFF_GATE + A_Q + B_W) // hw + h)),
            vec, vec, vec, vec,
            pl.BlockSpec((1, hw), lambda b, h, i: (0, 0)),
        ],
        out_specs=pl.BlockSpec((tq, hw), lambda b, h, i: (b * nq + i, h)),
        out_shape=jax.ShapeDtypeStruct((batch * seq_len, C_V), BF16),
        scratch_shapes=[
            pltpu.VMEM((2, HEAD_DIM, tq), BF16),
            pltpu.VMEM((n_kv, hw + ONES_ROWS, tk), BF16),
            pltpu.VMEM((2, 2, tk, tq), F32),
            pltpu.VMEM((1, tq), F32), pltpu.VMEM((hw + ONES_ROWS, tq), F32),
            pltpu.VMEM((1, tq), F32), pltpu.VMEM((hw + ONES_ROWS, tq), F32),
        ],
        compiler_params=pltpu.CompilerParams(
            dimension_semantics=("parallel", "parallel", "arbitrary"),
            vmem_limit_bytes=VMEM_LIMIT_BYTES),
        name="attn_c",
    )(proj, proj, proj, proj, lq1.reshape(1, HEAD_DIM), lk1.reshape(1, HEAD_DIM),
      lq2.reshape(1, HEAD_DIM), lk2.reshape(1, HEAD_DIM), subln_g.reshape(1, hw))


def _out_proj_kernel(a_ref, b_ref, c_ref, wa_ref, wb_ref, wc_ref, x_ref, g_ref, o_ref, y_scr,
                     *, n_col_tiles):
    j = pl.program_id(1)
    y = jnp.dot(a_ref[...], wa_ref[...], preferred_element_type=F32)
    y = y + jnp.dot(b_ref[...], wb_ref[...], preferred_element_type=F32)
    y = y + jnp.dot(c_ref[...], wc_ref[...], preferred_element_type=F32)
    y_scr[j] = y

    @pl.when(j == n_col_tiles - 1)
    def _():
        ss = jnp.zeros((y_scr.shape[1], 1), F32)
        for jj in range(n_col_tiles):
            t = y_scr[jj]
            ss = ss + jnp.sum(t * t, axis=-1, keepdims=True)
        inv = lax.rsqrt(ss * (1.0 / D_MODEL) + EPS)
        for jj in range(n_col_tiles):
            sl = slice(jj * OUT_TN, (jj + 1) * OUT_TN)
            o_ref[:, sl] = x_ref[:, sl] + (y_scr[jj] * inv) * g_ref[:, sl]


def _out_proj(mix_a, mix_b, mix_c, wa, wb, wc, x2d, post_g):
    tokens = x2d.shape[0]
    tm, tn = OUT_TM, OUT_TN
    assert tokens % tm == 0 and D_MODEL % tn == 0
    n_col_tiles = D_MODEL // tn
    kernel = functools.partial(_out_proj_kernel, n_col_tiles=n_col_tiles)
    return pl.pallas_call(
        kernel,
        grid=(tokens // tm, n_col_tiles),
        in_specs=[
            pl.BlockSpec((tm, A_Q), lambda i, j: (i, 0)),
            pl.BlockSpec((tm, B_W), lambda i, j: (i, 0)),
            pl.BlockSpec((tm, C_V), lambda i, j: (i, 0)),
            pl.BlockSpec((A_Q, tn), lambda i, j: (0, j)),
            pl.BlockSpec((B_W, tn), lambda i, j: (0, j)),
            pl.BlockSpec((C_V, tn), lambda i, j: (0, j)),
            pl.BlockSpec((tm, D_MODEL), lambda i, j: (i, 0)),
            pl.BlockSpec((1, D_MODEL), lambda i, j: (0, 0)),
        ],
        out_specs=pl.BlockSpec((tm, D_MODEL), lambda i, j: (i, 0)),
        out_shape=jax.ShapeDtypeStruct((tokens, D_MODEL), F32),
        scratch_shapes=[pltpu.VMEM((n_col_tiles, tm, tn), F32)],
        compiler_params=pltpu.CompilerParams(
            dimension_semantics=("parallel", "arbitrary"),
            vmem_limit_bytes=VMEM_LIMIT_BYTES),
        name="out_proj",
    )(mix_a, mix_b, mix_c, wa, wb, wc, x2d, post_g.reshape(1, D_MODEL))


def _angles(pos, dim, theta):
    inv = jnp.power(theta, -jnp.arange(0, dim, 2, dtype=F32) / dim)
    return pos.astype(F32)[:, None] * inv[None, :]


def _rope_tables(seq_len):
    t = jnp.arange(seq_len)
    ang_row = _angles(t // GRID_W, HEAD_DIM // 2, AXIAL_THETA)
    ang_col = _angles(t % GRID_W, HEAD_DIM // 2, AXIAL_THETA)
    cr, sr, cc_, sc_ = jnp.cos(ang_row), jnp.sin(ang_row), jnp.cos(ang_col), jnp.sin(ang_col)
    cos_ax = jnp.concatenate([cr, cr, cc_, cc_], axis=-1)
    sin_ax = jnp.concatenate([-sr, sr, -sc_, sc_], axis=-1)
    ang_t = _angles(t, ROPE_DIMS, ROPE_THETA)
    ct, st = jnp.cos(ang_t), jnp.sin(ang_t)
    rest = HEAD_DIM - ROPE_DIMS
    cos_p = jnp.concatenate([ct, ct, jnp.ones((seq_len, rest), F32)], axis=-1)
    sin_p = jnp.concatenate([-st, st, jnp.zeros((seq_len, rest), F32)], axis=-1)
    return cos_ax, sin_ax, cos_p, sin_p


def _trunk(x, params, w_in_bf, w_out_parts, bias_bands):
    batch, seq_len, _ = x.shape
    tables = _rope_tables(seq_len)
    x2d = x.reshape(batch * seq_len, D_MODEL)
    depth = w_in_bf.shape[0]
    for l in range(depth):
        lam_init = 0.8 - 0.6 * math.exp(-0.3 * l)
        proj = _in_proj(x2d, params["pre_norm_g"][l], w_in_bf[l], params["a_q_norm_g"][l],
                        params["a_k_norm_g"][l], tables, seq_len)
        mix_a = _attn_a(proj, batch, seq_len)
        mix_b = _attn_b(proj, bias_bands[l], batch, seq_len)
        mix_c = _attn_c(proj, params["c_lambda_q1"][l], params["c_lambda_k1"][l],
                        params["c_lambda_q2"][l], params["c_lambda_k2"][l],
                        params["c_subln_g"][l], lam_init, batch, seq_len)
        wa, wb, wc = w_out_parts[l]
        x2d = _out_proj(mix_a, mix_b, mix_c, wa, wb, wc, x2d, params["post_norm_g"][l])
    return x2d.reshape(batch, seq_len, D_MODEL)


def kernel(x_prompt, x_sample, pre_norm_g, post_norm_g, w_in, w_out, a_q_norm_g, a_k_norm_g,
           b_rel_bias, c_lambda_q1, c_lambda_k1, c_lambda_q2, c_lambda_k2, c_subln_g):
    params = dict(pre_norm_g=pre_norm_g, post_norm_g=post_norm_g, a_q_norm_g=a_q_norm_g,
                  a_k_norm_g=a_k_norm_g, c_lambda_q1=c_lambda_q1, c_lambda_k1=c_lambda_k1,
                  c_lambda_q2=c_lambda_q2, c_lambda_k2=c_lambda_k2, c_subln_g=c_subln_g)
    depth = w_in.shape[0]
    w_in_bf = w_in.astype(BF16)
    w_out_bf = w_out.astype(BF16)
    w_out_parts = [(w_out_bf[l, :A_Q], w_out_bf[l, A_Q:A_Q + B_W], w_out_bf[l, A_Q + B_W:])
                   for l in range(depth)]
    bias_bands = [_bias_bands(b_rel_bias[l]) for l in range(depth)]
    y_prompt = _trunk(x_prompt, params, w_in_bf, w_out_parts, bias_bands)
    y_sample = _trunk(x_sample, params, w_in_bf, w_out_parts, bias_bands)
    return (y_prompt, y_sample)
```

```python
import functools
import math

import jax
import jax.numpy as jnp
from jax import lax
from jax.experimental import pallas as pl
from jax.experimental.pallas import tpu as pltpu

F32 = jnp.float32
BF16 = jnp.bfloat16

D_MODEL = 4096
HEAD_DIM = 128
GRID_W = 64
A_HEADS = 12
A_KV_HEADS = 4
A_GROUP = A_HEADS // A_KV_HEADS
B_HEADS = 8
C_HEADS = 6
A_Q = A_HEADS * HEAD_DIM
A_KV = A_KV_HEADS * HEAD_DIM
B_W = B_HEADS * HEAD_DIM
C_QK = C_HEADS * 2 * HEAD_DIM
C_V = C_HEADS * 2 * HEAD_DIM
D_MIX = A_Q + B_W + C_V
D_IN = A_Q + 2 * A_KV + 3 * B_W + 2 * C_QK + C_V + D_MIX
NA_ROWS = 8
NA_COLS = 16
AXIAL_THETA = 10000.0
ROPE_THETA = 500000.0
ROPE_DIMS = HEAD_DIM // 4
EPS = 1e-6
NEG_INF = -1e30
SCALE = 1.0 / math.sqrt(HEAD_DIM)
SCALE_LOG2 = SCALE * math.log2(math.e)
ONES_ROWS = 16

OFF_QA = 0
OFF_KA = OFF_QA + A_Q
OFF_VA = OFF_KA + A_KV
OFF_QB = OFF_VA + A_KV
OFF_KB = OFF_QB + B_W
OFF_VB = OFF_KB + B_W
OFF_QC = OFF_VB + B_W
OFF_KC = OFF_QC + C_QK
OFF_VC = OFF_KC + C_QK
OFF_GATE = OFF_VC + C_V

VMEM_LIMIT_BYTES = 56 * 1024 * 1024

IN_TM = 512
IN_TN = 512
OUT_TM = 512
OUT_TN = 1024
A_TQ = 256
A_TK = 512
B_TQ = 1024
B_QROWS = 4
B_WROWS = B_QROWS + NA_ROWS
C_TQ = 512
C_TK = 512


def _swap_halves(x, h):
    lane = lax.broadcasted_iota(jnp.int32, x.shape, 1)
    first = (lane % (2 * h)) < h
    return jnp.where(first, pltpu.roll(x, HEAD_DIM - h, 1), pltpu.roll(x, h, 1))


def _silu(g):
    return g / (1.0 + jnp.exp(-g))


def _in_proj_kernel(x_ref, g_ref, w_ref, qn_ref, kn_ref, ca_ref, sa_ref, cc_ref, sc_ref,
                    o_ref, h_scr, acc_scr, *, n_tiles):
    j = pl.program_id(1)
    cur = j % 2
    prev = 1 - cur
    t = j - 1
    heads = IN_TN // HEAD_DIM

    @pl.when(j == 0)
    def _():
        x = x_ref[...]
        ms = jnp.sum(x * x, axis=-1, keepdims=True) * (1.0 / D_MODEL)
        h_scr[...] = ((x * lax.rsqrt(ms + EPS)) * g_ref[...]).astype(BF16)

    def matmul():
        acc_scr[cur] = jnp.dot(h_scr[...], w_ref[...], preferred_element_type=F32)

    def head_norm(y, g):
        ms = jnp.sum(y * y, axis=-1, keepdims=True) * (1.0 / HEAD_DIM)
        return (y * lax.rsqrt(ms + EPS)) * g

    def axial(y):
        return y * ca_ref[...] + _swap_halves(y, HEAD_DIM // 4) * sa_ref[...]

    def partial(y):
        return y * cc_ref[...] + _swap_halves(y, ROPE_DIMS // 2) * sc_ref[...]

    def finish(fn):
        for hh in range(heads):
            sl = slice(hh * HEAD_DIM, (hh + 1) * HEAD_DIM)
            o_ref[:, sl] = fn(acc_scr[prev, :, sl]).astype(BF16)

    t_qa = OFF_KA // IN_TN
    t_ka = OFF_VA // IN_TN
    t_qb0, t_qb1 = OFF_QB // IN_TN, OFF_KB // IN_TN
    t_qc0, t_kc0, t_vc0 = OFF_QC // IN_TN, OFF_KC // IN_TN, OFF_VC // IN_TN

    is_first = j == 0
    is_last = j == n_tiles
    is_qa = jnp.logical_and(t >= 0, t < t_qa)
    is_ka = jnp.logical_and(t >= t_qa, t < t_ka)
    is_qb = jnp.logical_and(t >= t_qb0, t < t_qb1)
    is_qc = jnp.logical_and(t >= t_qc0, t < t_kc0)
    is_kc = jnp.logical_and(t >= t_kc0, t < t_vc0)
    is_plain = jnp.logical_not(is_first | is_last | is_qa | is_ka | is_qb | is_qc | is_kc)

    def step(fn):
        finish(fn)
        matmul()

    pl.when(is_first)(matmul)
    pl.when(is_qa)(lambda: step(lambda y: axial(head_norm(y, qn_ref[...])) * SCALE_LOG2))
    pl.when(is_ka)(lambda: step(lambda y: axial(head_norm(y, kn_ref[...]))))
    pl.when(is_qb)(lambda: step(lambda y: y * SCALE_LOG2))
    pl.when(is_qc)(lambda: step(lambda y: partial(y) * SCALE_LOG2))
    pl.when(is_kc)(lambda: step(partial))
    pl.when(is_plain)(lambda: step(lambda y: y))
    pl.when(is_last)(lambda: finish(lambda y: y))


def _in_proj(x2d, pre_g, w_in_bf, qn_g, kn_g, tables, seq_len):
    tokens = x2d.shape[0]
    tm = IN_TM
    assert tokens % tm == 0 and seq_len % tm == 0 and D_IN % IN_TN == 0
    for off in (OFF_KA, OFF_VA, OFF_QB, OFF_KB, OFF_QC, OFF_KC, OFF_VC):
        assert off % IN_TN == 0
    n_tiles = D_IN // IN_TN
    assert OFF_GATE <= (n_tiles - 1) * IN_TN
    pos_blocks = seq_len // tm
    tab_spec = pl.BlockSpec((tm, HEAD_DIM), lambda i, j: (i % pos_blocks, 0))
    vec_spec = pl.BlockSpec((1, HEAD_DIM), lambda i, j: (0, 0))
    return pl.pallas_call(
        functools.partial(_in_proj_kernel, n_tiles=n_tiles),
        grid=(tokens // tm, n_tiles + 1),
        in_specs=[
            pl.BlockSpec((tm, D_MODEL), lambda i, j: (i, 0)),
            pl.BlockSpec((1, D_MODEL), lambda i, j: (0, 0)),
            pl.BlockSpec((D_MODEL, IN_TN), lambda i, j: (0, jnp.minimum(j, n_tiles - 1))),
            vec_spec, vec_spec, tab_spec, tab_spec, tab_spec, tab_spec,
        ],
        out_specs=pl.BlockSpec((tm, IN_TN), lambda i, j: (i, jnp.maximum(j - 1, 0))),
        out_shape=jax.ShapeDtypeStruct((tokens, D_IN), BF16),
        scratch_shapes=[pltpu.VMEM((tm, D_MODEL), BF16), pltpu.VMEM((2, tm, IN_TN), F32)],
        compiler_params=pltpu.CompilerParams(
            dimension_semantics=("parallel", "arbitrary"),
            vmem_limit_bytes=VMEM_LIMIT_BYTES),
        name="in_proj",
    )(x2d, pre_g.reshape(1, D_MODEL), w_in_bf, qn_g.reshape(1, HEAD_DIM),
      kn_g.reshape(1, HEAD_DIM), *tables)


def _build_vt(v_ref, vt_scr, *, n_kv, tk, dv):
    def chunk(c, carry):
        off = pl.multiple_of(c * tk, tk)
        vt_scr[c, 0:dv, :] = v_ref[pl.ds(off, tk), :].astype(F32).T.astype(BF16)
        row = lax.broadcasted_iota(jnp.int32, (ONES_ROWS, tk), 0)
        vt_scr[c, dv:dv + ONES_ROWS, :] = jnp.where(row == 0, 1.0, 0.0).astype(BF16)
        return carry

    lax.fori_loop(0, n_kv, chunk, 0)


def _flash_step(s, vt, m_scr, acc_scr):
    m_old = m_scr[...]
    m_new = jnp.maximum(m_old, jnp.max(s, axis=0, keepdims=True))
    alpha = jnp.exp2(m_old - m_new)
    p = jnp.exp2(s - m_new).astype(BF16)
    acc_scr[...] = alpha * acc_scr[...] + jnp.dot(vt, p, preferred_element_type=F32)
    m_scr[...] = m_new


def _attn_a_kernel(q_ref, k_ref, v_ref, g0_ref, g1_ref, g2_ref, o_ref,
                   qt_scr, vt_scr, s_scr, m_scr, acc_scr, *, tq, tk, n_kv):
    gate_refs = (g0_ref, g1_ref, g2_ref)

    @pl.when(pl.program_id(2) == 0)
    def _():
        _build_vt(v_ref, vt_scr, n_kv=n_kv, tk=tk, dv=HEAD_DIM)

    for g in range(A_GROUP):
        qt_scr[:, g * tq:(g + 1) * tq] = (
            q_ref[:, g * HEAD_DIM:(g + 1) * HEAD_DIM].astype(F32).T.astype(BF16))
    m_scr[...] = jnp.full(m_scr.shape, NEG_INF, F32)
    acc_scr[...] = jnp.zeros(acc_scr.shape, F32)

    def scores(j):
        start = pl.multiple_of(j * tk, tk)
        return jnp.dot(k_ref[pl.ds(start, tk), :], qt_scr[...], preferred_element_type=F32)

    s_scr[0] = scores(0)

    def body(jj, carry):
        for slot in range(2):
            j = 2 * jj + slot
            s_scr[1 - slot] = scores(jnp.minimum(j + 1, n_kv - 1))
            _flash_step(s_scr[slot], vt_scr[j], m_scr, acc_scr)
        return carry

    lax.fori_loop(0, n_kv // 2, body, 0)
    acc = acc_scr[...]
    out = (acc[0:HEAD_DIM] / acc[HEAD_DIM:HEAD_DIM + 1]).T
    for g in range(A_GROUP):
        sl = slice(g * HEAD_DIM, (g + 1) * HEAD_DIM)
        gate = gate_refs[g][...].astype(F32)
        o_ref[:, sl] = (out[g * tq:(g + 1) * tq, :] * _silu(gate)).astype(BF16)


def _attn_a(proj, batch, seq_len):
    tq, tk = A_TQ, A_TK
    assert seq_len % tq == 0 and seq_len % (2 * tk) == 0
    nq = seq_len // tq
    n_kv = seq_len // tk
    gw = A_GROUP * HEAD_DIM
    kernel = functools.partial(_attn_a_kernel, tq=tq, tk=tk, n_kv=n_kv)

    def gate_spec(g):
        return pl.BlockSpec(
            (tq, HEAD_DIM),
            lambda b, h, i: (b * nq + i, OFF_GATE // HEAD_DIM + A_GROUP * h + g))

    return pl.pallas_call(
        kernel,
        grid=(batch, A_KV_HEADS, nq),
        in_specs=[
            pl.BlockSpec((tq, gw), lambda b, h, i: (b * nq + i, h)),
            pl.BlockSpec((seq_len, HEAD_DIM), lambda b, h, i: (b, OFF_KA // HEAD_DIM + h)),
            pl.BlockSpec((seq_len, HEAD_DIM), lambda b, h, i: (b, OFF_VA // HEAD_DIM + h)),
            gate_spec(0), gate_spec(1), gate_spec(2),
        ],
        out_specs=pl.BlockSpec((tq, gw), lambda b, h, i: (b * nq + i, h)),
        out_shape=jax.ShapeDtypeStruct((batch * seq_len, A_Q), BF16),
        scratch_shapes=[
            pltpu.VMEM((HEAD_DIM, A_GROUP * tq), BF16),
            pltpu.VMEM((n_kv, HEAD_DIM + ONES_ROWS, tk), BF16),
            pltpu.VMEM((2, tk, A_GROUP * tq), F32),
            pltpu.VMEM((1, A_GROUP * tq), F32),
            pltpu.VMEM((HEAD_DIM + ONES_ROWS, A_GROUP * tq), F32),
        ],
        compiler_params=pltpu.CompilerParams(
            dimension_semantics=("parallel", "parallel", "arbitrary"),
            vmem_limit_bytes=VMEM_LIMIT_BYTES),
        name="attn_a",
    )(proj, proj, proj, proj, proj, proj)


def _attn_b_kernel(q_ref, k_ref, v_ref, gate_ref, bias_ref, o_ref, vt_scr,
                   *, blocks_per_step, grid_rows):
    t = pl.program_id(2)
    bq = B_QROWS * GRID_W
    wk = B_WROWS * GRID_W
    n_blocks = grid_rows // B_QROWS

    @pl.when(t == 0)
    def _():
        _build_vt(v_ref, vt_scr, n_kv=n_blocks, tk=bq, dv=HEAD_DIM)

    for u in range(blocks_per_step):
        rows = slice(u * bq, (u + 1) * bq)
        blk = t * blocks_per_step + u
        win_row = jnp.clip(blk * B_QROWS - NA_ROWS // 2, 0, grid_rows - B_WROWS)
        case = jnp.where(blk == 0, 0, jnp.where(blk == n_blocks - 1, 2, 1))
        chunk0 = win_row // B_QROWS
        qt = q_ref[rows, :].astype(F32).T.astype(BF16)
        k = k_ref[pl.ds(pl.multiple_of(win_row * GRID_W, bq), wk), :]
        s = jnp.dot(k, qt, preferred_element_type=F32) + bias_ref[case]
        p = jnp.exp2(s - jnp.max(s, axis=0, keepdims=True)).astype(BF16)
        vt = jnp.concatenate([vt_scr[chunk0 + w] for w in range(B_WROWS // B_QROWS)], axis=1)
        acc = jnp.dot(vt, p, preferred_element_type=F32)
        o = (acc[0:HEAD_DIM] / acc[HEAD_DIM:HEAD_DIM + 1]).T
        o_ref[rows, :] = (o * _silu(gate_ref[rows, :].astype(F32))).astype(BF16)


def _attn_b(proj, bias_blocks, batch, seq_len):
    tq = min(B_TQ, seq_len)
    bq = B_QROWS * GRID_W
    assert seq_len % tq == 0 and tq % bq == 0
    grid_rows = seq_len // GRID_W
    assert grid_rows % B_QROWS == 0 and grid_rows >= 4 * B_QROWS
    nq = seq_len // tq
    n_blocks = grid_rows // B_QROWS
    kernel = functools.partial(_attn_b_kernel, blocks_per_step=tq // bq, grid_rows=grid_rows)
    return pl.pallas_call(
        kernel,
        grid=(batch, B_HEADS, nq),
        in_specs=[
            pl.BlockSpec((tq, HEAD_DIM), lambda b, h, i: (b * nq + i, OFF_QB // HEAD_DIM + h)),
            pl.BlockSpec((seq_len, HEAD_DIM), lambda b, h, i: (b, OFF_KB // HEAD_DIM + h)),
            pl.BlockSpec((seq_len, HEAD_DIM), lambda b, h, i: (b, OFF_VB // HEAD_DIM + h)),
            pl.BlockSpec((tq, HEAD_DIM),
                         lambda b, h, i: (b * nq + i, (OFF_GATE + A_Q) // HEAD_DIM + h)),
            pl.BlockSpec((None, 3, B_WROWS * GRID_W, bq), lambda b, h, i: (h, 0, 0, 0)),
        ],
        out_specs=pl.BlockSpec((tq, HEAD_DIM), lambda b, h, i: (b * nq + i, h)),
        out_shape=jax.ShapeDtypeStruct((batch * seq_len, B_W), BF16),
        scratch_shapes=[pltpu.VMEM((n_blocks, HEAD_DIM + ONES_ROWS, bq), BF16)],
        compiler_params=pltpu.CompilerParams(
            dimension_semantics=("parallel", "parallel", "arbitrary"),
            vmem_limit_bytes=VMEM_LIMIT_BYTES),
        name="attn_b",
    )(proj, proj, proj, proj, bias_blocks)


def _bias_blocks(rel_bias):
    heads = rel_bias.shape[0]
    c = jnp.arange(GRID_W)
    col_start = jnp.clip(c - NA_COLS // 2, 0, GRID_W - NA_COLS)
    col_ok = (c[None, :] >= col_start[:, None]) & (c[None, :] < col_start[:, None] + NA_COLS)
    dc = jnp.clip(c[None, :] - c[:, None], -(NA_COLS - 1), NA_COLS - 1) + (NA_COLS - 1)
    any_rows = 8 * B_QROWS
    tables = []
    for first_row, win_row in ((0, 0), (2 * B_QROWS, B_QROWS), (any_rows - B_QROWS,
                                                                 any_rows - B_WROWS)):
        r = first_row + jnp.arange(B_QROWS)
        kr = win_row + jnp.arange(B_WROWS)
        band0 = jnp.clip(r - NA_ROWS // 2, 0, any_rows - NA_ROWS)
        row_ok = (kr[None, :] >= band0[:, None]) & (kr[None, :] < band0[:, None] + NA_ROWS)
        dr = jnp.clip(kr[None, :] - r[:, None] + (NA_ROWS - 1), 0, 2 * NA_ROWS - 2)
        bias = rel_bias[:, dr[:, None, :, None], dc[None, :, None, :]].astype(F32)
        ok = row_ok[:, None, :, None] & col_ok[None, :, None, :]
        bias = jnp.where(ok[None], bias * math.log2(math.e), NEG_INF)
        tables.append(jnp.transpose(bias, (0, 3, 4, 1, 2)).reshape(
            heads, B_WROWS * GRID_W, B_QROWS * GRID_W))
    return jnp.stack(tables, axis=1)


def _attn_c_kernel(q_ref, k_ref, v_ref, gate_ref, lq1_ref, lk1_ref, lq2_ref, lk2_ref, sg_ref,
                   o_ref, qt_scr, vt_scr, s_scr, m1_scr, a1_scr, m2_scr, a2_scr,
                   *, tk, n_kv, lam_init):
    dv = 2 * HEAD_DIM
    streams = ((m1_scr, a1_scr), (m2_scr, a2_scr))

    @pl.when(pl.program_id(2) == 0)
    def _():
        _build_vt(v_ref, vt_scr, n_kv=n_kv, tk=tk, dv=dv)

    for u, (m_scr, a_scr) in enumerate(streams):
        qt_scr[u] = q_ref[:, u * HEAD_DIM:(u + 1) * HEAD_DIM].astype(F32).T.astype(BF16)
        m_scr[...] = jnp.full(m_scr.shape, NEG_INF, F32)
        a_scr[...] = jnp.zeros(a_scr.shape, F32)

    def scores(j, u):
        start = pl.multiple_of(j * tk, tk)
        k = k_ref[pl.ds(start, tk), u * HEAD_DIM:(u + 1) * HEAD_DIM]
        return jnp.dot(k, qt_scr[u], preferred_element_type=F32)

    for u in range(2):
        s_scr[0, u] = scores(0, u)

    def body(jj, carry):
        for slot in range(2):
            j = 2 * jj + slot
            nxt = jnp.minimum(j + 1, n_kv - 1)
            for u, (m_scr, a_scr) in enumerate(streams):
                s_scr[1 - slot, u] = scores(nxt, u)
                _flash_step(s_scr[slot, u], vt_scr[j], m_scr, a_scr)
        return carry

    lax.fori_loop(0, n_kv // 2, body, 0)

    lam = (jnp.exp(jnp.sum(lq1_ref[...] * lk1_ref[...], axis=-1, keepdims=True))
           - jnp.exp(jnp.sum(lq2_ref[...] * lk2_ref[...], axis=-1, keepdims=True))
           + lam_init)
    a1 = a1_scr[...]
    a2 = a2_scr[...]
    o = (a1[0:dv] / a1[dv:dv + 1] - lam * (a2[0:dv] / a2[dv:dv + 1])).T
    ms = jnp.sum(o * o, axis=-1, keepdims=True) * (1.0 / (2 * HEAD_DIM))
    o = (o * lax.rsqrt(ms + EPS)) * sg_ref[...] * (1.0 - lam_init)
    o_ref[...] = (o * _silu(gate_ref[...].astype(F32))).astype(BF16)


def _attn_c(proj, lq1, lk1, lq2, lk2, subln_g, lam_init, batch, seq_len):
    tq, tk = C_TQ, C_TK
    assert seq_len % tq == 0 and seq_len % (2 * tk) == 0
    nq = seq_len // tq
    n_kv = seq_len // tk
    hw = 2 * HEAD_DIM
    kernel = functools.partial(_attn_c_kernel, tk=tk, n_kv=n_kv, lam_init=lam_init)
    vec = pl.BlockSpec((1, HEAD_DIM), lambda b, h, i: (0, 0))
    return pl.pallas_call(
        kernel,
        grid=(batch, C_HEADS, nq),
        in_specs=[
            pl.BlockSpec((tq, hw), lambda b, h, i: (b * nq + i, OFF_QC // hw + h)),
            pl.BlockSpec((seq_len, hw), lambda b, h, i: (b, OFF_KC // hw + h)),
            pl.BlockSpec((seq_len, hw), lambda b, h, i: (b, OFF_VC // hw + h)),
            pl.BlockSpec((tq, hw), lambda b, h, i: (b * nq + i, (OFF_GATE + A_Q + B_W) // hw + h)),
            vec, vec, vec, vec,
            pl.BlockSpec((1, hw), lambda b, h, i: (0, 0)),
        ],
        out_specs=pl.BlockSpec((tq, hw), lambda b, h, i: (b * nq + i, h)),
        out_shape=jax.ShapeDtypeStruct((batch * seq_len, C_V), BF16),
        scratch_shapes=[
            pltpu.VMEM((2, HEAD_DIM, tq), BF16),
            pltpu.VMEM((n_kv, hw + ONES_ROWS, tk), BF16),
            pltpu.VMEM((2, 2, tk, tq), F32),
            pltpu.VMEM((1, tq), F32), pltpu.VMEM((hw + ONES_ROWS, tq), F32),
            pltpu.VMEM((1, tq), F32), pltpu.VMEM((hw + ONES_ROWS, tq), F32),
        ],
        compiler_params=pltpu.CompilerParams(
            dimension_semantics=("parallel", "parallel", "arbitrary"),
            vmem_limit_bytes=VMEM_LIMIT_BYTES),
        name="attn_c",
    )(proj, proj, proj, proj, lq1.reshape(1, HEAD_DIM), lk1.reshape(1, HEAD_DIM),
      lq2.reshape(1, HEAD_DIM), lk2.reshape(1, HEAD_DIM), subln_g.reshape(1, hw))


def _out_proj_kernel(a_ref, b_ref, c_ref, wa_ref, wb_ref, wc_ref, x_ref, g_ref, o_ref,
                     y_scr, ss_scr, *, n_row_tiles, n_col_tiles):
    i = pl.program_id(0)
    j = pl.program_id(1)
    cur = i % 2
    prev = 1 - cur

    def matmul():
        y = jnp.dot(a_ref[...], wa_ref[...], preferred_element_type=F32)
        y = y + jnp.dot(b_ref[...], wb_ref[...], preferred_element_type=F32)
        y = y + jnp.dot(c_ref[...], wc_ref[...], preferred_element_type=F32)
        y_scr[cur, j] = y
        ss_scr[cur, j] = jnp.sum(y * y, axis=-1, keepdims=True)

    def finish():
        ss = ss_scr[prev, 0]
        for jj in range(1, n_col_tiles):
            ss = ss + ss_scr[prev, jj]
        inv = lax.rsqrt(ss * (1.0 / D_MODEL) + EPS)
        o_ref[...] = x_ref[...] + (y_scr[prev, j] * inv) * g_ref[...]

    pl.when(i == 0)(matmul)

    @pl.when(jnp.logical_and(i > 0, i < n_row_tiles))
    def _():
        finish()
        matmul()

    pl.when(i == n_row_tiles)(finish)


def _out_proj(mix_a, mix_b, mix_c, wa, wb, wc, x2d, post_g):
    tokens = x2d.shape[0]
    tm, tn = OUT_TM, OUT_TN
    assert tokens % tm == 0 and D_MODEL % tn == 0
    n_row_tiles = tokens // tm
    n_col_tiles = D_MODEL // tn
    kernel = functools.partial(_out_proj_kernel, n_row_tiles=n_row_tiles, n_col_tiles=n_col_tiles)

    def lhs_row(i, j):
        return (jnp.minimum(i, n_row_tiles - 1), 0)

    def done_tile(i, j):
        return (jnp.maximum(i - 1, 0), jnp.where(i == 0, 0, j))

    return pl.pallas_call(
        kernel,
        grid=(n_row_tiles + 1, n_col_tiles),
        in_specs=[
            pl.BlockSpec((tm, A_Q), lhs_row),
            pl.BlockSpec((tm, B_W), lhs_row),
            pl.BlockSpec((tm, C_V), lhs_row),
            pl.BlockSpec((A_Q, tn), lambda i, j: (0, j)),
            pl.BlockSpec((B_W, tn), lambda i, j: (0, j)),
            pl.BlockSpec((C_V, tn), lambda i, j: (0, j)),
            pl.BlockSpec((tm, tn), done_tile),
            pl.BlockSpec((1, tn), lambda i, j: (0, j)),
        ],
        out_specs=pl.BlockSpec((tm, tn), done_tile),
        out_shape=jax.ShapeDtypeStruct((tokens, D_MODEL), F32),
        scratch_shapes=[pltpu.VMEM((2, n_col_tiles, tm, tn), F32),
                        pltpu.VMEM((2, n_col_tiles, tm, 1), F32)],
        compiler_params=pltpu.CompilerParams(
            dimension_semantics=("arbitrary", "arbitrary"),
            vmem_limit_bytes=VMEM_LIMIT_BYTES),
        name="out_proj",
    )(mix_a, mix_b, mix_c, wa, wb, wc, x2d, post_g.reshape(1, D_MODEL))


def _angles(pos, dim, theta):
    inv = jnp.power(theta, -jnp.arange(0, dim, 2, dtype=F32) / dim)
    return pos.astype(F32)[:, None] * inv[None, :]


def _rope_tables(seq_len):
    t = jnp.arange(seq_len)
    ang_row = _angles(t // GRID_W, HEAD_DIM // 2, AXIAL_THETA)
    ang_col = _angles(t % GRID_W, HEAD_DIM // 2, AXIAL_THETA)
    cr, sr, cc_, sc_ = jnp.cos(ang_row), jnp.sin(ang_row), jnp.cos(ang_col), jnp.sin(ang_col)
    cos_ax = jnp.concatenate([cr, cr, cc_, cc_], axis=-1)
    sin_ax = jnp.concatenate([-sr, sr, -sc_, sc_], axis=-1)
    ang_t = _angles(t, ROPE_DIMS, ROPE_THETA)
    ct, st = jnp.cos(ang_t), jnp.sin(ang_t)
    rest = HEAD_DIM - ROPE_DIMS
    cos_p = jnp.concatenate([ct, ct, jnp.ones((seq_len, rest), F32)], axis=-1)
    sin_p = jnp.concatenate([-st, st, jnp.zeros((seq_len, rest), F32)], axis=-1)
    return cos_ax, sin_ax, cos_p, sin_p


def _trunk(x, params, w_in_bf, w_out_parts, bias_blocks):
    batch, seq_len, _ = x.shape
    tables = _rope_tables(seq_len)
    x2d = x.reshape(batch * seq_len, D_MODEL)
    depth = w_in_bf.shape[0]
    for l in range(depth):
        lam_init = 0.8 - 0.6 * math.exp(-0.3 * l)
        proj = _in_proj(x2d, params["pre_norm_g"][l], w_in_bf[l], params["a_q_norm_g"][l],
                        params["a_k_norm_g"][l], tables, seq_len)
        mix_a = _attn_a(proj, batch, seq_len)
        mix_b = _attn_b(proj, bias_blocks[l], batch, seq_len)
        mix_c = _attn_c(proj, params["c_lambda_q1"][l], params["c_lambda_k1"][l],
                        params["c_lambda_q2"][l], params["c_lambda_k2"][l],
                        params["c_subln_g"][l], lam_init, batch, seq_len)
        wa, wb, wc = w_out_parts[l]
        x2d = _out_proj(mix_a, mix_b, mix_c, wa, wb, wc, x2d, params["post_norm_g"][l])
    return x2d.reshape(batch, seq_len, D_MODEL)


def _prepare(params):
    depth = params["w_in"].shape[0]
    w_in_bf = params["w_in"].astype(BF16)
    w_out_bf = params["w_out"].astype(BF16)
    w_out_parts = [(w_out_bf[l, :A_Q], w_out_bf[l, A_Q:A_Q + B_W], w_out_bf[l, A_Q + B_W:])
                   for l in range(depth)]
    bias_blocks = [_bias_blocks(params["b_rel_bias"][l]) for l in range(depth)]
    return w_in_bf, w_out_parts, bias_blocks


def kernel(x_prompt, x_sample, pre_norm_g, post_norm_g, w_in, w_out, a_q_norm_g, a_k_norm_g,
           b_rel_bias, c_lambda_q1, c_lambda_k1, c_lambda_q2, c_lambda_k2, c_subln_g):
    params = dict(pre_norm_g=pre_norm_g, post_norm_g=post_norm_g, a_q_norm_g=a_q_norm_g,
                  a_k_norm_g=a_k_norm_g, c_lambda_q1=c_lambda_q1, c_lambda_k1=c_lambda_k1,
                  c_lambda_q2=c_lambda_q2, c_lambda_k2=c_lambda_k2, c_subln_g=c_subln_g)
    params.update(w_in=w_in, w_out=w_out, b_rel_bias=b_rel_bias)
    prepared = _prepare(params)
    y_prompt = _trunk(x_prompt, params, *prepared)
    y_sample = _trunk(x_sample, params, *prepared)
    return (y_prompt, y_sample)
```

```python
import functools
import math

import jax
import jax.numpy as jnp
import numpy as np
from jax import lax
from jax.experimental import pallas as pl
from jax.experimental.pallas import tpu as pltpu

F32 = jnp.float32
BF16 = jnp.bfloat16

D_MODEL = 4096
HEAD_DIM = 128
GRID_W = 64
A_HEADS = 12
A_KV_HEADS = 4
A_GROUP = A_HEADS // A_KV_HEADS
B_HEADS = 8
C_HEADS = 6
A_Q = A_HEADS * HEAD_DIM
A_KV = A_KV_HEADS * HEAD_DIM
B_W = B_HEADS * HEAD_DIM
C_QK = C_HEADS * 2 * HEAD_DIM
C_V = C_HEADS * 2 * HEAD_DIM
D_MIX = A_Q + B_W + C_V
D_IN = A_Q + 2 * A_KV + 3 * B_W + 2 * C_QK + C_V + D_MIX
NA_ROWS = 8
NA_COLS = 16
AXIAL_THETA = 10000.0
ROPE_THETA = 500000.0
ROPE_DIMS = HEAD_DIM // 4
EPS = 1e-6
NEG_INF = -1e30
SCALE = 1.0 / math.sqrt(HEAD_DIM)
SCALE_LOG2 = SCALE * math.log2(math.e)
ONES_ROWS = 16

OFF_QA = 0
OFF_KA = OFF_QA + A_Q
OFF_VA = OFF_KA + A_KV
OFF_QB = OFF_VA + A_KV
OFF_KB = OFF_QB + B_W
OFF_VB = OFF_KB + B_W
OFF_QC = OFF_VB + B_W
OFF_KC = OFF_QC + C_QK
OFF_VC = OFF_KC + C_QK
OFF_GATE = OFF_VC + C_V

VMEM_LIMIT_BYTES = 56 * 1024 * 1024

IN_TM = 512
IN_TN = 512
OUT_TM = 512
OUT_TN = 1024
A_TQ = 256
A_TK = 512
B_TQ = 1024
B_QROWS = 4
B_WROWS = B_QROWS + NA_ROWS
C_TQ = 512
C_TK = 512


def _swap_halves(x, h):
    lane = lax.broadcasted_iota(jnp.int32, x.shape, 1)
    first = (lane % (2 * h)) < h
    return jnp.where(first, pltpu.roll(x, HEAD_DIM - h, 1), pltpu.roll(x, h, 1))


def _silu(g):
    return g / (1.0 + jnp.exp(-g))


def _in_proj_kernel(x_ref, g_ref, w_ref, qn_ref, kn_ref, ca_ref, sa_ref, cc_ref, sc_ref,
                    o_ref, h_scr, acc_scr, *, n_tiles):
    j = pl.program_id(1)
    cur = j % 2
    prev = 1 - cur
    t = j - 1
    heads = IN_TN // HEAD_DIM

    @pl.when(j == 0)
    def _():
        x = x_ref[...]
        ms = jnp.sum(x * x, axis=-1, keepdims=True) * (1.0 / D_MODEL)
        h_scr[...] = ((x * lax.rsqrt(ms + EPS)) * g_ref[...]).astype(BF16)

    def matmul():
        acc_scr[cur] = jnp.dot(h_scr[...], w_ref[...], preferred_element_type=F32)

    def head_norm(y, g):
        ms = jnp.sum(y * y, axis=-1, keepdims=True) * (1.0 / HEAD_DIM)
        return (y * lax.rsqrt(ms + EPS)) * g

    def axial(y):
        return y * ca_ref[...] + _swap_halves(y, HEAD_DIM // 4) * sa_ref[...]

    def partial(y):
        return y * cc_ref[...] + _swap_halves(y, ROPE_DIMS // 2) * sc_ref[...]

    def finish(fn):
        for hh in range(heads):
            sl = slice(hh * HEAD_DIM, (hh + 1) * HEAD_DIM)
            o_ref[:, sl] = fn(acc_scr[prev, :, sl]).astype(BF16)

    t_qa = OFF_KA // IN_TN
    t_ka = OFF_VA // IN_TN
    t_qb0, t_qb1 = OFF_QB // IN_TN, OFF_KB // IN_TN
    t_qc0, t_kc0, t_vc0 = OFF_QC // IN_TN, OFF_KC // IN_TN, OFF_VC // IN_TN

    is_first = j == 0
    is_last = j == n_tiles
    is_qa = jnp.logical_and(t >= 0, t < t_qa)
    is_ka = jnp.logical_and(t >= t_qa, t < t_ka)
    is_qb = jnp.logical_and(t >= t_qb0, t < t_qb1)
    is_qc = jnp.logical_and(t >= t_qc0, t < t_kc0)
    is_kc = jnp.logical_and(t >= t_kc0, t < t_vc0)
    is_plain = jnp.logical_not(is_first | is_last | is_qa | is_ka | is_qb | is_qc | is_kc)

    def step(fn):
        finish(fn)
        matmul()

    pl.when(is_first)(matmul)
    pl.when(is_qa)(lambda: step(lambda y: axial(head_norm(y, qn_ref[...])) * SCALE_LOG2))
    pl.when(is_ka)(lambda: step(lambda y: axial(head_norm(y, kn_ref[...]))))
    pl.when(is_qb)(lambda: step(lambda y: y * SCALE_LOG2))
    pl.when(is_qc)(lambda: step(lambda y: partial(y) * SCALE_LOG2))
    pl.when(is_kc)(lambda: step(partial))
    pl.when(is_plain)(lambda: step(lambda y: y))
    pl.when(is_last)(lambda: finish(lambda y: y))


def _in_proj(x2d, pre_g, w_in_bf, qn_g, kn_g, tables, seq_len):
    tokens = x2d.shape[0]
    tm = IN_TM
    assert tokens % tm == 0 and seq_len % tm == 0 and D_IN % IN_TN == 0
    for off in (OFF_KA, OFF_VA, OFF_QB, OFF_KB, OFF_QC, OFF_KC, OFF_VC):
        assert off % IN_TN == 0
    n_tiles = D_IN // IN_TN
    assert OFF_GATE <= (n_tiles - 1) * IN_TN
    pos_blocks = seq_len // tm
    tab_spec = pl.BlockSpec((tm, HEAD_DIM), lambda i, j: (i % pos_blocks, 0))
    vec_spec = pl.BlockSpec((1, HEAD_DIM), lambda i, j: (0, 0))
    return pl.pallas_call(
        functools.partial(_in_proj_kernel, n_tiles=n_tiles),
        grid=(tokens // tm, n_tiles + 1),
        in_specs=[
            pl.BlockSpec((tm, D_MODEL), lambda i, j: (i, 0)),
            pl.BlockSpec((1, D_MODEL), lambda i, j: (0, 0)),
            pl.BlockSpec((D_MODEL, IN_TN), lambda i, j: (0, jnp.minimum(j, n_tiles - 1))),
            vec_spec, vec_spec, tab_spec, tab_spec, tab_spec, tab_spec,
        ],
        out_specs=pl.BlockSpec((tm, IN_TN), lambda i, j: (i, jnp.maximum(j - 1, 0))),
        out_shape=jax.ShapeDtypeStruct((tokens, D_IN), BF16),
        scratch_shapes=[pltpu.VMEM((tm, D_MODEL), BF16), pltpu.VMEM((2, tm, IN_TN), F32)],
        compiler_params=pltpu.CompilerParams(
            dimension_semantics=("parallel", "arbitrary"),
            vmem_limit_bytes=VMEM_LIMIT_BYTES),
        name="in_proj",
    )(x2d, pre_g.reshape(1, D_MODEL), w_in_bf, qn_g.reshape(1, HEAD_DIM),
      kn_g.reshape(1, HEAD_DIM), *tables)


def _build_vt(v_ref, vt_scr, *, n_kv, tk, dv):
    def chunk(c, carry):
        off = pl.multiple_of(c * tk, tk)
        vt_scr[c, 0:dv, :] = v_ref[pl.ds(off, tk), :].astype(F32).T.astype(BF16)
        row = lax.broadcasted_iota(jnp.int32, (ONES_ROWS, tk), 0)
        vt_scr[c, dv:dv + ONES_ROWS, :] = jnp.where(row == 0, 1.0, 0.0).astype(BF16)
        return carry

    lax.fori_loop(0, n_kv, chunk, 0)


def _flash_step(s, vt, m_scr, acc_scr):
    m_old = m_scr[...]
    m_new = jnp.maximum(m_old, jnp.max(s, axis=0, keepdims=True))
    alpha = jnp.exp2(m_old - m_new)
    p = jnp.exp2(s - m_new).astype(BF16)
    acc_scr[...] = alpha * acc_scr[...] + jnp.dot(vt, p, preferred_element_type=F32)
    m_scr[...] = m_new


def _attn_a_kernel(q_ref, k_ref, v_ref, g0_ref, g1_ref, g2_ref, o_ref,
                   qt_scr, vt_scr, s_scr, m_scr, acc_scr, *, tq, tk, n_kv):
    gate_refs = (g0_ref, g1_ref, g2_ref)

    @pl.when(pl.program_id(2) == 0)
    def _():
        _build_vt(v_ref, vt_scr, n_kv=n_kv, tk=tk, dv=HEAD_DIM)

    for g in range(A_GROUP):
        qt_scr[:, g * tq:(g + 1) * tq] = (
            q_ref[:, g * HEAD_DIM:(g + 1) * HEAD_DIM].astype(F32).T.astype(BF16))
    m_scr[...] = jnp.full(m_scr.shape, NEG_INF, F32)
    acc_scr[...] = jnp.zeros(acc_scr.shape, F32)

    def scores(j):
        start = pl.multiple_of(j * tk, tk)
        return jnp.dot(k_ref[pl.ds(start, tk), :], qt_scr[...], preferred_element_type=F32)

    s_scr[0] = scores(0)

    def step_pair(jj, is_tail):
        for slot in range(2):
            j = 2 * jj + slot
            if not (is_tail and slot == 1):
                s_scr[1 - slot] = scores(j + 1)
            _flash_step(s_scr[slot], vt_scr[j], m_scr, acc_scr)

    def body(jj, carry):
        step_pair(jj, False)
        return carry

    lax.fori_loop(0, n_kv // 2 - 1, body, 0)
    step_pair(n_kv // 2 - 1, True)
    acc = acc_scr[...]
    out = (acc[0:HEAD_DIM] / acc[HEAD_DIM:HEAD_DIM + 1]).T
    for g in range(A_GROUP):
        sl = slice(g * HEAD_DIM, (g + 1) * HEAD_DIM)
        gate = gate_refs[g][...].astype(F32)
        o_ref[:, sl] = (out[g * tq:(g + 1) * tq, :] * _silu(gate)).astype(BF16)


def _attn_a(proj, batch, seq_len):
    tq, tk = A_TQ, A_TK
    assert seq_len % tq == 0 and seq_len % (2 * tk) == 0
    nq = seq_len // tq
    n_kv = seq_len // tk
    gw = A_GROUP * HEAD_DIM
    kernel = functools.partial(_attn_a_kernel, tq=tq, tk=tk, n_kv=n_kv)

    def gate_spec(g):
        return pl.BlockSpec(
            (tq, HEAD_DIM),
            lambda b, h, i: (b * nq + i, OFF_GATE // HEAD_DIM + A_GROUP * h + g))

    return pl.pallas_call(
        kernel,
        grid=(batch, A_KV_HEADS, nq),
        in_specs=[
            pl.BlockSpec((tq, gw), lambda b, h, i: (b * nq + i, h)),
            pl.BlockSpec((seq_len, HEAD_DIM), lambda b, h, i: (b, OFF_KA // HEAD_DIM + h)),
            pl.BlockSpec((seq_len, HEAD_DIM), lambda b, h, i: (b, OFF_VA // HEAD_DIM + h)),
            gate_spec(0), gate_spec(1), gate_spec(2),
        ],
        out_specs=pl.BlockSpec((tq, gw), lambda b, h, i: (b * nq + i, h)),
        out_shape=jax.ShapeDtypeStruct((batch * seq_len, A_Q), BF16),
        scratch_shapes=[
            pltpu.VMEM((HEAD_DIM, A_GROUP * tq), BF16),
            pltpu.VMEM((n_kv, HEAD_DIM + ONES_ROWS, tk), BF16),
            pltpu.VMEM((2, tk, A_GROUP * tq), F32),
            pltpu.VMEM((1, A_GROUP * tq), F32),
            pltpu.VMEM((HEAD_DIM + ONES_ROWS, A_GROUP * tq), F32),
        ],
        compiler_params=pltpu.CompilerParams(
            dimension_semantics=("parallel", "parallel", "arbitrary"),
            vmem_limit_bytes=VMEM_LIMIT_BYTES),
        name="attn_a",
    )(proj, proj, proj, proj, proj, proj)


def _attn_b_kernel(q_ref, k_ref, v_ref, gate_ref, bias_ref, o_ref, vt_scr,
                   *, blocks_per_step, grid_rows):
    t = pl.program_id(2)
    bq = B_QROWS * GRID_W
    wk = B_WROWS * GRID_W
    n_blocks = grid_rows // B_QROWS

    @pl.when(t == 0)
    def _():
        _build_vt(v_ref, vt_scr, n_kv=n_blocks, tk=bq, dv=HEAD_DIM)

    for u in range(blocks_per_step):
        rows = slice(u * bq, (u + 1) * bq)
        blk = t * blocks_per_step + u
        win_row = jnp.clip(blk * B_QROWS - NA_ROWS // 2, 0, grid_rows - B_WROWS)
        case = jnp.where(blk == 0, 0, jnp.where(blk == n_blocks - 1, 2, 1))
        chunk0 = win_row // B_QROWS
        qt = q_ref[rows, :].astype(F32).T.astype(BF16)
        k = k_ref[pl.ds(pl.multiple_of(win_row * GRID_W, bq), wk), :]
        s = jnp.dot(k, qt, preferred_element_type=F32) + bias_ref[case]
        p = jnp.exp2(s - jnp.max(s, axis=0, keepdims=True)).astype(BF16)
        vt = jnp.concatenate([vt_scr[chunk0 + w] for w in range(B_WROWS // B_QROWS)], axis=1)
        acc = jnp.dot(vt, p, preferred_element_type=F32)
        o = (acc[0:HEAD_DIM] / acc[HEAD_DIM:HEAD_DIM + 1]).T
        o_ref[rows, :] = (o * _silu(gate_ref[rows, :].astype(F32))).astype(BF16)


def _attn_b(proj, bias_blocks, batch, seq_len):
    tq = min(B_TQ, seq_len)
    bq = B_QROWS * GRID_W
    assert seq_len % tq == 0 and tq % bq == 0
    grid_rows = seq_len // GRID_W
    assert grid_rows % B_QROWS == 0 and grid_rows >= 4 * B_QROWS
    nq = seq_len // tq
    n_blocks = grid_rows // B_QROWS
    kernel = functools.partial(_attn_b_kernel, blocks_per_step=tq // bq, grid_rows=grid_rows)
    return pl.pallas_call(
        kernel,
        grid=(batch, B_HEADS, nq),
        in_specs=[
            pl.BlockSpec((tq, HEAD_DIM), lambda b, h, i: (b * nq + i, OFF_QB // HEAD_DIM + h)),
            pl.BlockSpec((seq_len, HEAD_DIM), lambda b, h, i: (b, OFF_KB // HEAD_DIM + h)),
            pl.BlockSpec((seq_len, HEAD_DIM), lambda b, h, i: (b, OFF_VB // HEAD_DIM + h)),
            pl.BlockSpec((tq, HEAD_DIM),
                         lambda b, h, i: (b * nq + i, (OFF_GATE + A_Q) // HEAD_DIM + h)),
            pl.BlockSpec((None, 3, B_WROWS * GRID_W, bq), lambda b, h, i: (h, 0, 0, 0)),
        ],
        out_specs=pl.BlockSpec((tq, HEAD_DIM), lambda b, h, i: (b * nq + i, h)),
        out_shape=jax.ShapeDtypeStruct((batch * seq_len, B_W), BF16),
        scratch_shapes=[pltpu.VMEM((n_blocks, HEAD_DIM + ONES_ROWS, bq), BF16)],
        compiler_params=pltpu.CompilerParams(
            dimension_semantics=("parallel", "parallel", "arbitrary"),
            vmem_limit_bytes=VMEM_LIMIT_BYTES),
        name="attn_b",
    )(proj, proj, proj, proj, bias_blocks)


def _bias_blocks(rel_bias):
    heads = rel_bias.shape[0]
    n_dr, n_dc = 2 * NA_ROWS - 1, 2 * NA_COLS - 1
    c = np.arange(GRID_W)
    col_start = np.clip(c - NA_COLS // 2, 0, GRID_W - NA_COLS)
    col_ok = (c[None, :] >= col_start[:, None]) & (c[None, :] < col_start[:, None] + NA_COLS)
    dc = np.clip(c[None, :] - c[:, None], -(NA_COLS - 1), NA_COLS - 1) + (NA_COLS - 1)
    pick_dc = (dc.reshape(-1, 1) == np.arange(n_dc)[None, :]).astype(np.float32)
    any_rows = 8 * B_QROWS
    tables = []
    for first_row, win_row in ((0, 0), (2 * B_QROWS, B_QROWS), (any_rows - B_QROWS,
                                                                 any_rows - B_WROWS)):
        r = first_row + np.arange(B_QROWS)
        kr = win_row + np.arange(B_WROWS)
        band0 = np.clip(r - NA_ROWS // 2, 0, any_rows - NA_ROWS)
        row_ok = (kr[None, :] >= band0[:, None]) & (kr[None, :] < band0[:, None] + NA_ROWS)
        dr = np.clip(kr[None, :] - r[:, None] + (NA_ROWS - 1), 0, n_dr - 1)
        pick_dr = (dr.reshape(-1, 1) == np.arange(n_dr)[None, :]).astype(np.float32)
        bias = jnp.einsum("hij,pi,qj->hpq", rel_bias.astype(F32), pick_dr, pick_dc,
                          precision=lax.Precision.HIGHEST)
        bias = bias.reshape(heads, B_QROWS, B_WROWS, GRID_W, GRID_W)
        ok = row_ok[:, :, None, None] & col_ok[None, None, :, :]
        bias = jnp.where(ok[None], bias * math.log2(math.e), NEG_INF)
        tables.append(jnp.transpose(bias, (0, 2, 4, 1, 3)).reshape(
            heads, B_WROWS * GRID_W, B_QROWS * GRID_W))
    return jnp.stack(tables, axis=1)


def _attn_c_kernel(q_ref, k_ref, v_ref, gate_ref, lq1_ref, lk1_ref, lq2_ref, lk2_ref, sg_ref,
                   o_ref, qt_scr, vt_scr, s_scr, m1_scr, a1_scr, m2_scr, a2_scr,
                   *, tk, n_kv, lam_init):
    dv = 2 * HEAD_DIM
    streams = ((m1_scr, a1_scr), (m2_scr, a2_scr))

    @pl.when(pl.program_id(2) == 0)
    def _():
        _build_vt(v_ref, vt_scr, n_kv=n_kv, tk=tk, dv=dv)

    for u, (m_scr, a_scr) in enumerate(streams):
        qt_scr[u] = q_ref[:, u * HEAD_DIM:(u + 1) * HEAD_DIM].astype(F32).T.astype(BF16)
        m_scr[...] = jnp.full(m_scr.shape, NEG_INF, F32)
        a_scr[...] = jnp.zeros(a_scr.shape, F32)

    def scores(j, u):
        start = pl.multiple_of(j * tk, tk)
        k = k_ref[pl.ds(start, tk), u * HEAD_DIM:(u + 1) * HEAD_DIM]
        return jnp.dot(k, qt_scr[u], preferred_element_type=F32)

    for u in range(2):
        s_scr[0, u] = scores(0, u)

    def step_pair(jj, is_tail):
        for slot in range(2):
            j = 2 * jj + slot
            for u, (m_scr, a_scr) in enumerate(streams):
                if not (is_tail and slot == 1):
                    s_scr[1 - slot, u] = scores(j + 1, u)
                _flash_step(s_scr[slot, u], vt_scr[j], m_scr, a_scr)

    def body(jj, carry):
        step_pair(jj, False)
        return carry

    lax.fori_loop(0, n_kv // 2 - 1, body, 0)
    step_pair(n_kv // 2 - 1, True)

    lam = (jnp.exp(jnp.sum(lq1_ref[...] * lk1_ref[...], axis=-1, keepdims=True))
           - jnp.exp(jnp.sum(lq2_ref[...] * lk2_ref[...], axis=-1, keepdims=True))
           + lam_init)
    a1 = a1_scr[...]
    a2 = a2_scr[...]
    o = (a1[0:dv] / a1[dv:dv + 1] - lam * (a2[0:dv] / a2[dv:dv + 1])).T
    ms = jnp.sum(o * o, axis=-1, keepdims=True) * (1.0 / (2 * HEAD_DIM))
    o = (o * lax.rsqrt(ms + EPS)) * sg_ref[...] * (1.0 - lam_init)
    o_ref[...] = (o * _silu(gate_ref[...].astype(F32))).astype(BF16)


def _attn_c(proj, lq1, lk1, lq2, lk2, subln_g, lam_init, batch, seq_len):
    tq, tk = C_TQ, C_TK
    assert seq_len % tq == 0 and seq_len % (2 * tk) == 0
    nq = seq_len // tq
    n_kv = seq_len // tk
    hw = 2 * HEAD_DIM
    kernel = functools.partial(_attn_c_kernel, tk=tk, n_kv=n_kv, lam_init=lam_init)
    vec = pl.BlockSpec((1, HEAD_DIM), lambda b, h, i: (0, 0))
    return pl.pallas_call(
        kernel,
        grid=(batch, C_HEADS, nq),
        in_specs=[
            pl.BlockSpec((tq, hw), lambda b, h, i: (b * nq + i, OFF_QC // hw + h)),
            pl.BlockSpec((seq_len, hw), lambda b, h, i: (b, OFF_KC // hw + h)),
            pl.BlockSpec((seq_len, hw), lambda b, h, i: (b, OFF_VC // hw + h)),
            pl.BlockSpec((tq, hw), lambda b, h, i: (b * nq + i, (OFF_GATE + A_Q + B_W) // hw + h)),
            vec, vec, vec, vec,
            pl.BlockSpec((1, hw), lambda b, h, i: (0, 0)),
        ],
        out_specs=pl.BlockSpec((tq, hw), lambda b, h, i: (b * nq + i, h)),
        out_shape=jax.ShapeDtypeStruct((batch * seq_len, C_V), BF16),
        scratch_shapes=[
            pltpu.VMEM((2, HEAD_DIM, tq), BF16),
            pltpu.VMEM((n_kv, hw + ONES_ROWS, tk), BF16),
            pltpu.VMEM((2, 2, tk, tq), F32),
            pltpu.VMEM((1, tq), F32), pltpu.VMEM((hw + ONES_ROWS, tq), F32),
            pltpu.VMEM((1, tq), F32), pltpu.VMEM((hw + ONES_ROWS, tq), F32),
        ],
        compiler_params=pltpu.CompilerParams(
            dimension_semantics=("parallel", "parallel", "arbitrary"),
            vmem_limit_bytes=VMEM_LIMIT_BYTES),
        name="attn_c",
    )(proj, proj, proj, proj, lq1.reshape(1, HEAD_DIM), lk1.reshape(1, HEAD_DIM),
      lq2.reshape(1, HEAD_DIM), lk2.reshape(1, HEAD_DIM), subln_g.reshape(1, hw))


def _out_proj_kernel(a_ref, b_ref, c_ref, w_ref, x_ref, g_ref, o_ref,
                     y_scr, ss_scr, *, n_row_tiles, n_col_tiles):
    i = pl.program_id(0)
    j = pl.program_id(1)
    cur = i % 2
    prev = 1 - cur

    def matmul():
        y = jnp.dot(a_ref[...], w_ref[0:A_Q, :], preferred_element_type=F32)
        y = y + jnp.dot(b_ref[...], w_ref[A_Q:A_Q + B_W, :], preferred_element_type=F32)
        y = y + jnp.dot(c_ref[...], w_ref[A_Q + B_W:D_MIX, :], preferred_element_type=F32)
        y_scr[cur, j] = y
        ss_scr[cur, j] = jnp.sum(y * y, axis=-1, keepdims=True)

    def finish():
        ss = ss_scr[prev, 0]
        for jj in range(1, n_col_tiles):
            ss = ss + ss_scr[prev, jj]
        inv = lax.rsqrt(ss * (1.0 / D_MODEL) + EPS)
        o_ref[...] = x_ref[...] + (y_scr[prev, j] * inv) * g_ref[...]

    pl.when(i == 0)(matmul)

    @pl.when(jnp.logical_and(i > 0, i < n_row_tiles))
    def _():
        finish()
        matmul()

    pl.when(i == n_row_tiles)(finish)


def _out_proj(mix_a, mix_b, mix_c, w_out_bf, x2d, post_g):
    tokens = x2d.shape[0]
    tm, tn = OUT_TM, OUT_TN
    assert tokens % tm == 0 and D_MODEL % tn == 0
    n_row_tiles = tokens // tm
    n_col_tiles = D_MODEL // tn
    kernel = functools.partial(_out_proj_kernel, n_row_tiles=n_row_tiles, n_col_tiles=n_col_tiles)

    def lhs_row(i, j):
        return (jnp.minimum(i, n_row_tiles - 1), 0)

    def done_tile(i, j):
        return (jnp.maximum(i - 1, 0), jnp.where(i == 0, 0, j))

    return pl.pallas_call(
        kernel,
        grid=(n_row_tiles + 1, n_col_tiles),
        in_specs=[
            pl.BlockSpec((tm, A_Q), lhs_row),
            pl.BlockSpec((tm, B_W), lhs_row),
            pl.BlockSpec((tm, C_V), lhs_row),
            pl.BlockSpec((D_MIX, tn), lambda i, j: (0, j)),
            pl.BlockSpec((tm, tn), done_tile),
            pl.BlockSpec((1, tn), lambda i, j: (0, j)),
        ],
        out_specs=pl.BlockSpec((tm, tn), done_tile),
        out_shape=jax.ShapeDtypeStruct((tokens, D_MODEL), F32),
        scratch_shapes=[pltpu.VMEM((2, n_col_tiles, tm, tn), F32),
                        pltpu.VMEM((2, n_col_tiles, tm, 1), F32)],
        compiler_params=pltpu.CompilerParams(
            dimension_semantics=("arbitrary", "arbitrary"),
            vmem_limit_bytes=VMEM_LIMIT_BYTES),
        name="out_proj",
    )(mix_a, mix_b, mix_c, w_out_bf, x2d, post_g.reshape(1, D_MODEL))


def _angles(pos, dim, theta):
    inv = jnp.power(theta, -jnp.arange(0, dim, 2, dtype=F32) / dim)
    return pos.astype(F32)[:, None] * inv[None, :]


def _rope_tables(seq_len):
    t = jnp.arange(seq_len)
    ang_row = _angles(t // GRID_W, HEAD_DIM // 2, AXIAL_THETA)
    ang_col = _angles(t % GRID_W, HEAD_DIM // 2, AXIAL_THETA)
    cr, sr, cc_, sc_ = jnp.cos(ang_row), jnp.sin(ang_row), jnp.cos(ang_col), jnp.sin(ang_col)
    cos_ax = jnp.concatenate([cr, cr, cc_, cc_], axis=-1)
    sin_ax = jnp.concatenate([-sr, sr, -sc_, sc_], axis=-1)
    ang_t = _angles(t, ROPE_DIMS, ROPE_THETA)
    ct, st = jnp.cos(ang_t), jnp.sin(ang_t)
    rest = HEAD_DIM - ROPE_DIMS
    cos_p = jnp.concatenate([ct, ct, jnp.ones((seq_len, rest), F32)], axis=-1)
    sin_p = jnp.concatenate([-st, st, jnp.zeros((seq_len, rest), F32)], axis=-1)
    return cos_ax, sin_ax, cos_p, sin_p


def _trunk(x, params, w_in_bf, w_out_bf, bias_blocks):
    batch, seq_len, _ = x.shape
    tables = _rope_tables(seq_len)
    x2d = x.reshape(batch * seq_len, D_MODEL)
    depth = w_in_bf.shape[0]
    for l in range(depth):
        lam_init = 0.8 - 0.6 * math.exp(-0.3 * l)
        proj = _in_proj(x2d, params["pre_norm_g"][l], w_in_bf[l], params["a_q_norm_g"][l],
                        params["a_k_norm_g"][l], tables, seq_len)
        mix_a = _attn_a(proj, batch, seq_len)
        mix_b = _attn_b(proj, bias_blocks[l], batch, seq_len)
        mix_c = _attn_c(proj, params["c_lambda_q1"][l], params["c_lambda_k1"][l],
                        params["c_lambda_q2"][l], params["c_lambda_k2"][l],
                        params["c_subln_g"][l], lam_init, batch, seq_len)
        x2d = _out_proj(mix_a, mix_b, mix_c, w_out_bf[l], x2d, params["post_norm_g"][l])
    return x2d.reshape(batch, seq_len, D_MODEL)


def _prepare(params):
    depth = params["w_in"].shape[0]
    w_in_bf = params["w_in"].astype(BF16)
    w_out_bf = params["w_out"].astype(BF16)
    bias_blocks = [_bias_blocks(params["b_rel_bias"][l]) for l in range(depth)]
    return w_in_bf, w_out_bf, bias_blocks


def kernel(x_prompt, x_sample, pre_norm_g, post_norm_g, w_in, w_out, a_q_norm_g, a_k_norm_g,
           b_rel_bias, c_lambda_q1, c_lambda_k1, c_lambda_q2, c_lambda_k2, c_subln_g):
    params = dict(pre_norm_g=pre_norm_g, post_norm_g=post_norm_g, a_q_norm_g=a_q_norm_g,
                  a_k_norm_g=a_k_norm_g, c_lambda_q1=c_lambda_q1, c_lambda_k1=c_lambda_k1,
                  c_lambda_q2=c_lambda_q2, c_lambda_k2=c_lambda_k2, c_subln_g=c_subln_g)
    params.update(w_in=w_in, w_out=w_out, b_rel_bias=b_rel_bias)
    prepared = _prepare(params)
    y_prompt = _trunk(x_prompt, params, *prepared)
    y_sample = _trunk(x_sample, params, *prepared)
    return (y_prompt, y_sample)
```

```python
import functools
import math

import jax
import jax.numpy as jnp
import numpy as np
from jax import lax
from jax.experimental import pallas as pl
from jax.experimental.pallas import tpu as pltpu

F32 = jnp.float32
BF16 = jnp.bfloat16

D_MODEL = 4096
HEAD_DIM = 128
GRID_W = 64
A_HEADS = 12
A_KV_HEADS = 4
A_GROUP = A_HEADS // A_KV_HEADS
B_HEADS = 8
C_HEADS = 6
A_Q = A_HEADS * HEAD_DIM
A_KV = A_KV_HEADS * HEAD_DIM
B_W = B_HEADS * HEAD_DIM
C_QK = C_HEADS * 2 * HEAD_DIM
C_V = C_HEADS * 2 * HEAD_DIM
D_MIX = A_Q + B_W + C_V
D_IN = A_Q + 2 * A_KV + 3 * B_W + 2 * C_QK + C_V + D_MIX
NA_ROWS = 8
NA_COLS = 16
AXIAL_THETA = 10000.0
ROPE_THETA = 500000.0
ROPE_DIMS = HEAD_DIM // 4
EPS = 1e-6
NEG_INF = -1e30
SCALE = 1.0 / math.sqrt(HEAD_DIM)
SCALE_LOG2 = SCALE * math.log2(math.e)
ONES_ROWS = 16

OFF_QA = 0
OFF_KA = OFF_QA + A_Q
OFF_VA = OFF_KA + A_KV
OFF_QB = OFF_VA + A_KV
OFF_KB = OFF_QB + B_W
OFF_VB = OFF_KB + B_W
OFF_QC = OFF_VB + B_W
OFF_KC = OFF_QC + C_QK
OFF_VC = OFF_KC + C_QK
OFF_GATE = OFF_VC + C_V

VMEM_LIMIT_BYTES = 56 * 1024 * 1024

IN_TM = 512
IN_TN = 512
OUT_TM = 512
OUT_TN = 1024
A_TQ = 256
A_TK = 512
B_TQ = 1024
B_QROWS = 4
B_WROWS = B_QROWS + NA_ROWS
C_TQ = 512
C_TK = 512
KV_UNROLL = 8


def _swap_halves(x, h):
    lane = lax.broadcasted_iota(jnp.int32, x.shape, 1)
    first = (lane % (2 * h)) < h
    return jnp.where(first, pltpu.roll(x, HEAD_DIM - h, 1), pltpu.roll(x, h, 1))


def _silu(g):
    return g / (1.0 + jnp.exp(-g))


def _in_proj_kernel(x_ref, g_ref, w_ref, qn_ref, kn_ref, ca_ref, sa_ref, cc_ref, sc_ref,
                    o_ref, h_scr, acc_scr, *, n_tiles):
    j = pl.program_id(1)
    cur = j % 2
    prev = 1 - cur
    t = j - 1
    heads = IN_TN // HEAD_DIM

    @pl.when(j == 0)
    def _():
        x = x_ref[...]
        ms = jnp.sum(x * x, axis=-1, keepdims=True) * (1.0 / D_MODEL)
        h_scr[...] = ((x * lax.rsqrt(ms + EPS)) * g_ref[...]).astype(BF16)

    def matmul():
        acc_scr[cur] = jnp.dot(h_scr[...], w_ref[...], preferred_element_type=F32)

    def head_norm(y, g):
        ms = jnp.sum(y * y, axis=-1, keepdims=True) * (1.0 / HEAD_DIM)
        return (y * lax.rsqrt(ms + EPS)) * g

    def axial(y):
        return y * ca_ref[...] + _swap_halves(y, HEAD_DIM // 4) * sa_ref[...]

    def partial(y):
        return y * cc_ref[...] + _swap_halves(y, ROPE_DIMS // 2) * sc_ref[...]

    def finish(fn):
        for hh in range(heads):
            sl = slice(hh * HEAD_DIM, (hh + 1) * HEAD_DIM)
            o_ref[:, sl] = fn(acc_scr[prev, :, sl]).astype(BF16)

    t_qa = OFF_KA // IN_TN
    t_ka = OFF_VA // IN_TN
    t_qb0, t_qb1 = OFF_QB // IN_TN, OFF_KB // IN_TN
    t_qc0, t_kc0, t_vc0 = OFF_QC // IN_TN, OFF_KC // IN_TN, OFF_VC // IN_TN

    is_first = j == 0
    is_last = j == n_tiles
    is_qa = jnp.logical_and(t >= 0, t < t_qa)
    is_ka = jnp.logical_and(t >= t_qa, t < t_ka)
    is_qb = jnp.logical_and(t >= t_qb0, t < t_qb1)
    is_qc = jnp.logical_and(t >= t_qc0, t < t_kc0)
    is_kc = jnp.logical_and(t >= t_kc0, t < t_vc0)
    is_plain = jnp.logical_not(is_first | is_last | is_qa | is_ka | is_qb | is_qc | is_kc)

    def step(fn):
        finish(fn)
        matmul()

    pl.when(is_first)(matmul)
    pl.when(is_qa)(lambda: step(lambda y: axial(head_norm(y, qn_ref[...])) * SCALE_LOG2))
    pl.when(is_ka)(lambda: step(lambda y: axial(head_norm(y, kn_ref[...]))))
    pl.when(is_qb)(lambda: step(lambda y: y * SCALE_LOG2))
    pl.when(is_qc)(lambda: step(lambda y: partial(y) * SCALE_LOG2))
    pl.when(is_kc)(lambda: step(partial))
    pl.when(is_plain)(lambda: step(lambda y: y))
    pl.when(is_last)(lambda: finish(lambda y: y))


def _in_proj(x2d, pre_g, w_in_bf, layer, qn_g, kn_g, tables, seq_len):
    tokens = x2d.shape[0]
    tm = IN_TM
    assert tokens % tm == 0 and seq_len % tm == 0 and D_IN % IN_TN == 0
    for off in (OFF_KA, OFF_VA, OFF_QB, OFF_KB, OFF_QC, OFF_KC, OFF_VC):
        assert off % IN_TN == 0
    n_tiles = D_IN // IN_TN
    assert OFF_GATE <= (n_tiles - 1) * IN_TN
    pos_blocks = seq_len // tm
    tab_spec = pl.BlockSpec((tm, HEAD_DIM), lambda i, j: (i % pos_blocks, 0))
    vec_spec = pl.BlockSpec((1, HEAD_DIM), lambda i, j: (0, 0))
    return pl.pallas_call(
        functools.partial(_in_proj_kernel, n_tiles=n_tiles),
        grid=(tokens // tm, n_tiles + 1),
        in_specs=[
            pl.BlockSpec((tm, D_MODEL), lambda i, j: (i, 0)),
            pl.BlockSpec((1, D_MODEL), lambda i, j: (0, 0)),
            pl.BlockSpec((None, D_MODEL, IN_TN),
                         lambda i, j: (layer, 0, jnp.minimum(j, n_tiles - 1))),
            vec_spec, vec_spec, tab_spec, tab_spec, tab_spec, tab_spec,
        ],
        out_specs=pl.BlockSpec((tm, IN_TN), lambda i, j: (i, jnp.maximum(j - 1, 0))),
        out_shape=jax.ShapeDtypeStruct((tokens, D_IN), BF16),
        scratch_shapes=[pltpu.VMEM((tm, D_MODEL), BF16), pltpu.VMEM((2, tm, IN_TN), F32)],
        compiler_params=pltpu.CompilerParams(
            dimension_semantics=("parallel", "arbitrary"),
            vmem_limit_bytes=VMEM_LIMIT_BYTES),
        name="in_proj",
    )(x2d, pre_g.reshape(1, D_MODEL), w_in_bf, qn_g.reshape(1, HEAD_DIM),
      kn_g.reshape(1, HEAD_DIM), *tables)


def _build_vt(v_ref, vt_scr, *, n_kv, tk, dv):
    def chunk(c, carry):
        off = pl.multiple_of(c * tk, tk)
        vt_scr[c, 0:dv, :] = v_ref[pl.ds(off, tk), :].astype(F32).T.astype(BF16)
        row = lax.broadcasted_iota(jnp.int32, (ONES_ROWS, tk), 0)
        vt_scr[c, dv:dv + ONES_ROWS, :] = jnp.where(row == 0, 1.0, 0.0).astype(BF16)
        return carry

    lax.fori_loop(0, n_kv, chunk, 0)


def _kv_unroll(n_kv):
    unroll = min(KV_UNROLL, n_kv)
    assert unroll % 2 == 0 and n_kv % unroll == 0
    return unroll


def _flash_step(s, vt, m_scr, acc_scr):
    m_old = m_scr[...]
    m_new = jnp.maximum(m_old, jnp.max(s, axis=0, keepdims=True))
    alpha = jnp.exp2(m_old - m_new)
    p = jnp.exp2(s - m_new).astype(BF16)
    acc_scr[...] = alpha * acc_scr[...] + jnp.dot(vt, p, preferred_element_type=F32)
    m_scr[...] = m_new


def _attn_a_kernel(q_ref, k_ref, v_ref, g0_ref, g1_ref, g2_ref, o_ref,
                   qt_scr, vt_scr, s_scr, m_scr, acc_scr, *, tq, tk, n_kv):
    gate_refs = (g0_ref, g1_ref, g2_ref)

    @pl.when(pl.program_id(2) == 0)
    def _():
        _build_vt(v_ref, vt_scr, n_kv=n_kv, tk=tk, dv=HEAD_DIM)

    for g in range(A_GROUP):
        qt_scr[:, g * tq:(g + 1) * tq] = (
            q_ref[:, g * HEAD_DIM:(g + 1) * HEAD_DIM].astype(F32).T.astype(BF16))
    m_scr[...] = jnp.full(m_scr.shape, NEG_INF, F32)
    acc_scr[...] = jnp.zeros(acc_scr.shape, F32)

    def scores(j):
        start = pl.multiple_of(j * tk, tk)
        return jnp.dot(k_ref[pl.ds(start, tk), :], qt_scr[...], preferred_element_type=F32)

    s_scr[0] = scores(0)

    unroll = _kv_unroll(n_kv)

    def step_group(jj, is_tail):
        for u in range(unroll):
            j = unroll * jj + u
            slot = u % 2
            if not (is_tail and u == unroll - 1):
                s_scr[1 - slot] = scores(j + 1)
            _flash_step(s_scr[slot], vt_scr[j], m_scr, acc_scr)

    def body(jj, carry):
        step_group(jj, False)
        return carry

    lax.fori_loop(0, n_kv // unroll - 1, body, 0)
    step_group(n_kv // unroll - 1, True)
    acc = acc_scr[...]
    out = (acc[0:HEAD_DIM] / acc[HEAD_DIM:HEAD_DIM + 1]).T
    for g in range(A_GROUP):
        sl = slice(g * HEAD_DIM, (g + 1) * HEAD_DIM)
        gate = gate_refs[g][...].astype(F32)
        o_ref[:, sl] = (out[g * tq:(g + 1) * tq, :] * _silu(gate)).astype(BF16)


def _attn_a(proj, batch, seq_len):
    tq, tk = A_TQ, A_TK
    assert seq_len % tq == 0 and seq_len % tk == 0
    nq = seq_len // tq
    n_kv = seq_len // tk
    gw = A_GROUP * HEAD_DIM
    kernel = functools.partial(_attn_a_kernel, tq=tq, tk=tk, n_kv=n_kv)

    def gate_spec(g):
        return pl.BlockSpec(
            (tq, HEAD_DIM),
            lambda b, h, i: (b * nq + i, OFF_GATE // HEAD_DIM + A_GROUP * h + g))

    return pl.pallas_call(
        kernel,
        grid=(batch, A_KV_HEADS, nq),
        in_specs=[
            pl.BlockSpec((tq, gw), lambda b, h, i: (b * nq + i, h)),
            pl.BlockSpec((seq_len, HEAD_DIM), lambda b, h, i: (b, OFF_KA // HEAD_DIM + h)),
            pl.BlockSpec((seq_len, HEAD_DIM), lambda b, h, i: (b, OFF_VA // HEAD_DIM + h)),
            gate_spec(0), gate_spec(1), gate_spec(2),
        ],
        out_specs=pl.BlockSpec((tq, gw), lambda b, h, i: (b * nq + i, h)),
        out_shape=jax.ShapeDtypeStruct((batch * seq_len, A_Q), BF16),
        scratch_shapes=[
            pltpu.VMEM((HEAD_DIM, A_GROUP * tq), BF16),
            pltpu.VMEM((n_kv, HEAD_DIM + ONES_ROWS, tk), BF16),
            pltpu.VMEM((2, tk, A_GROUP * tq), F32),
            pltpu.VMEM((1, A_GROUP * tq), F32),
            pltpu.VMEM((HEAD_DIM + ONES_ROWS, A_GROUP * tq), F32),
        ],
        compiler_params=pltpu.CompilerParams(
            dimension_semantics=("parallel", "parallel", "arbitrary"),
            vmem_limit_bytes=VMEM_LIMIT_BYTES),
        name="attn_a",
    )(proj, proj, proj, proj, proj, proj)


def _attn_b_kernel(q_ref, k_ref, v_ref, gate_ref, bias_ref, o_ref, vt_scr,
                   *, blocks_per_step, grid_rows):
    t = pl.program_id(2)
    bq = B_QROWS * GRID_W
    wk = B_WROWS * GRID_W
    n_blocks = grid_rows // B_QROWS

    @pl.when(t == 0)
    def _():
        _build_vt(v_ref, vt_scr, n_kv=n_blocks, tk=bq, dv=HEAD_DIM)

    for u in range(blocks_per_step):
        rows = slice(u * bq, (u + 1) * bq)
        blk = t * blocks_per_step + u
        win_row = jnp.clip(blk * B_QROWS - NA_ROWS // 2, 0, grid_rows - B_WROWS)
        case = jnp.where(blk == 0, 0, jnp.where(blk == n_blocks - 1, 2, 1))
        chunk0 = win_row // B_QROWS
        qt = q_ref[rows, :].astype(F32).T.astype(BF16)
        k = k_ref[pl.ds(pl.multiple_of(win_row * GRID_W, bq), wk), :]
        s = jnp.dot(k, qt, preferred_element_type=F32) + bias_ref[case]
        p = jnp.exp2(s - jnp.max(s, axis=0, keepdims=True)).astype(BF16)
        vt = jnp.concatenate([vt_scr[chunk0 + w] for w in range(B_WROWS // B_QROWS)], axis=1)
        acc = jnp.dot(vt, p, preferred_element_type=F32)
        o = (acc[0:HEAD_DIM] / acc[HEAD_DIM:HEAD_DIM + 1]).T
        o_ref[rows, :] = (o * _silu(gate_ref[rows, :].astype(F32))).astype(BF16)


def _attn_b(proj, bias_blocks, batch, seq_len):
    tq = min(B_TQ, seq_len)
    bq = B_QROWS * GRID_W
    assert seq_len % tq == 0 and tq % bq == 0
    grid_rows = seq_len // GRID_W
    assert grid_rows % B_QROWS == 0 and grid_rows >= 4 * B_QROWS
    nq = seq_len // tq
    n_blocks = grid_rows // B_QROWS
    kernel = functools.partial(_attn_b_kernel, blocks_per_step=tq // bq, grid_rows=grid_rows)
    return pl.pallas_call(
        kernel,
        grid=(batch, B_HEADS, nq),
        in_specs=[
            pl.BlockSpec((tq, HEAD_DIM), lambda b, h, i: (b * nq + i, OFF_QB // HEAD_DIM + h)),
            pl.BlockSpec((seq_len, HEAD_DIM), lambda b, h, i: (b, OFF_KB // HEAD_DIM + h)),
            pl.BlockSpec((seq_len, HEAD_DIM), lambda b, h, i: (b, OFF_VB // HEAD_DIM + h)),
            pl.BlockSpec((tq, HEAD_DIM),
                         lambda b, h, i: (b * nq + i, (OFF_GATE + A_Q) // HEAD_DIM + h)),
            pl.BlockSpec((None, 3, B_WROWS * GRID_W, bq), lambda b, h, i: (h, 0, 0, 0)),
        ],
        out_specs=pl.BlockSpec((tq, HEAD_DIM), lambda b, h, i: (b * nq + i, h)),
        out_shape=jax.ShapeDtypeStruct((batch * seq_len, B_W), BF16),
        scratch_shapes=[pltpu.VMEM((n_blocks, HEAD_DIM + ONES_ROWS, bq), BF16)],
        compiler_params=pltpu.CompilerParams(
            dimension_semantics=("parallel", "parallel", "arbitrary"),
            vmem_limit_bytes=VMEM_LIMIT_BYTES),
        name="attn_b",
    )(proj, proj, proj, proj, bias_blocks)


def _bias_blocks(rel_bias):
    heads = rel_bias.shape[0]
    n_dr, n_dc = 2 * NA_ROWS - 1, 2 * NA_COLS - 1
    c = np.arange(GRID_W)
    col_start = np.clip(c - NA_COLS // 2, 0, GRID_W - NA_COLS)
    col_ok = (c[None, :] >= col_start[:, None]) & (c[None, :] < col_start[:, None] + NA_COLS)
    dc = np.clip(c[None, :] - c[:, None], -(NA_COLS - 1), NA_COLS - 1) + (NA_COLS - 1)
    pick_dc = (dc.reshape(-1, 1) == np.arange(n_dc)[None, :]).astype(np.float32)
    any_rows = 8 * B_QROWS
    tables = []
    for first_row, win_row in ((0, 0), (2 * B_QROWS, B_QROWS), (any_rows - B_QROWS,
                                                                 any_rows - B_WROWS)):
        r = first_row + np.arange(B_QROWS)
        kr = win_row + np.arange(B_WROWS)
        band0 = np.clip(r - NA_ROWS // 2, 0, any_rows - NA_ROWS)
        row_ok = (kr[None, :] >= band0[:, None]) & (kr[None, :] < band0[:, None] + NA_ROWS)
        dr = np.clip(kr[None, :] - r[:, None] + (NA_ROWS - 1), 0, n_dr - 1)
        pick_dr = (dr.reshape(-1, 1) == np.arange(n_dr)[None, :]).astype(np.float32)
        bias = jnp.einsum("hij,pi,qj->hpq", rel_bias.astype(F32), pick_dr, pick_dc,
                          precision=lax.Precision.HIGHEST)
        bias = bias.reshape(heads, B_QROWS, B_WROWS, GRID_W, GRID_W)
        ok = row_ok[:, :, None, None] & col_ok[None, None, :, :]
        bias = jnp.where(ok[None], bias * math.log2(math.e), NEG_INF)
        tables.append(jnp.transpose(bias, (0, 2, 4, 1, 3)).reshape(
            heads, B_WROWS * GRID_W, B_QROWS * GRID_W))
    return jnp.stack(tables, axis=1)


def _attn_c_kernel(q_ref, k_ref, v_ref, gate_ref, lq1_ref, lk1_ref, lq2_ref, lk2_ref, sg_ref,
                   o_ref, qt_scr, vt_scr, s_scr, m1_scr, a1_scr, m2_scr, a2_scr,
                   *, tk, n_kv, lam_init):
    dv = 2 * HEAD_DIM
    streams = ((m1_scr, a1_scr), (m2_scr, a2_scr))

    @pl.when(pl.program_id(2) == 0)
    def _():
        _build_vt(v_ref, vt_scr, n_kv=n_kv, tk=tk, dv=dv)

    for u, (m_scr, a_scr) in enumerate(streams):
        qt_scr[u] = q_ref[:, u * HEAD_DIM:(u + 1) * HEAD_DIM].astype(F32).T.astype(BF16)
        m_scr[...] = jnp.full(m_scr.shape, NEG_INF, F32)
        a_scr[...] = jnp.zeros(a_scr.shape, F32)

    def scores(j, u):
        start = pl.multiple_of(j * tk, tk)
        k = k_ref[pl.ds(start, tk), u * HEAD_DIM:(u + 1) * HEAD_DIM]
        return jnp.dot(k, qt_scr[u], preferred_element_type=F32)

    for u in range(2):
        s_scr[0, u] = scores(0, u)

    unroll = _kv_unroll(n_kv)

    def step_group(jj, is_tail):
        for t in range(unroll):
            j = unroll * jj + t
            slot = t % 2
            for u, (m_scr, a_scr) in enumerate(streams):
                if not (is_tail and t == unroll - 1):
                    s_scr[1 - slot, u] = scores(j + 1, u)
                _flash_step(s_scr[slot, u], vt_scr[j], m_scr, a_scr)

    def body(jj, carry):
        step_group(jj, False)
        return carry

    lax.fori_loop(0, n_kv // unroll - 1, body, 0)
    step_group(n_kv // unroll - 1, True)

    lam = (jnp.exp(jnp.sum(lq1_ref[...] * lk1_ref[...], axis=-1, keepdims=True))
           - jnp.exp(jnp.sum(lq2_ref[...] * lk2_ref[...], axis=-1, keepdims=True))
           + lam_init)
    a1 = a1_scr[...]
    a2 = a2_scr[...]
    o = (a1[0:dv] / a1[dv:dv + 1] - lam * (a2[0:dv] / a2[dv:dv + 1])).T
    ms = jnp.sum(o * o, axis=-1, keepdims=True) * (1.0 / (2 * HEAD_DIM))
    o = (o * lax.rsqrt(ms + EPS)) * sg_ref[...] * (1.0 - lam_init)
    o_ref[...] = (o * _silu(gate_ref[...].astype(F32))).astype(BF16)


def _attn_c(proj, lq1, lk1, lq2, lk2, subln_g, lam_init, batch, seq_len):
    tq, tk = C_TQ, C_TK
    assert seq_len % tq == 0 and seq_len % tk == 0
    nq = seq_len // tq
    n_kv = seq_len // tk
    hw = 2 * HEAD_DIM
    kernel = functools.partial(_attn_c_kernel, tk=tk, n_kv=n_kv, lam_init=lam_init)
    vec = pl.BlockSpec((1, HEAD_DIM), lambda b, h, i: (0, 0))
    return pl.pallas_call(
        kernel,
        grid=(batch, C_HEADS, nq),
        in_specs=[
            pl.BlockSpec((tq, hw), lambda b, h, i: (b * nq + i, OFF_QC // hw + h)),
            pl.BlockSpec((seq_len, hw), lambda b, h, i: (b, OFF_KC // hw + h)),
            pl.BlockSpec((seq_len, hw), lambda b, h, i: (b, OFF_VC // hw + h)),
            pl.BlockSpec((tq, hw), lambda b, h, i: (b * nq + i, (OFF_GATE + A_Q + B_W) // hw + h)),
            vec, vec, vec, vec,
            pl.BlockSpec((1, hw), lambda b, h, i: (0, 0)),
        ],
        out_specs=pl.BlockSpec((tq, hw), lambda b, h, i: (b * nq + i, h)),
        out_shape=jax.ShapeDtypeStruct((batch * seq_len, C_V), BF16),
        scratch_shapes=[
            pltpu.VMEM((2, HEAD_DIM, tq), BF16),
            pltpu.VMEM((n_kv, hw + ONES_ROWS, tk), BF16),
            pltpu.VMEM((2, 2, tk, tq), F32),
            pltpu.VMEM((1, tq), F32), pltpu.VMEM((hw + ONES_ROWS, tq), F32),
            pltpu.VMEM((1, tq), F32), pltpu.VMEM((hw + ONES_ROWS, tq), F32),
        ],
        compiler_params=pltpu.CompilerParams(
            dimension_semantics=("parallel", "parallel", "arbitrary"),
            vmem_limit_bytes=VMEM_LIMIT_BYTES),
        name="attn_c",
    )(proj, proj, proj, proj, lq1.reshape(1, HEAD_DIM), lk1.reshape(1, HEAD_DIM),
      lq2.reshape(1, HEAD_DIM), lk2.reshape(1, HEAD_DIM), subln_g.reshape(1, hw))


def _out_proj_kernel(a_ref, b_ref, c_ref, w_ref, x_ref, g_ref, o_ref,
                     y_scr, ss_scr, *, n_row_tiles, n_col_tiles):
    i = pl.program_id(0)
    j = pl.program_id(1)
    cur = i % 2
    prev = 1 - cur

    def matmul():
        y = jnp.dot(a_ref[...], w_ref[0:A_Q, :], preferred_element_type=F32)
        y = y + jnp.dot(b_ref[...], w_ref[A_Q:A_Q + B_W, :], preferred_element_type=F32)
        y = y + jnp.dot(c_ref[...], w_ref[A_Q + B_W:D_MIX, :], preferred_element_type=F32)
        y_scr[cur, j] = y
        ss_scr[cur, j] = jnp.sum(y * y, axis=-1, keepdims=True)

    def finish():
        ss = ss_scr[prev, 0]
        for jj in range(1, n_col_tiles):
            ss = ss + ss_scr[prev, jj]
        inv = lax.rsqrt(ss * (1.0 / D_MODEL) + EPS)
        o_ref[...] = x_ref[...] + (y_scr[prev, j] * inv) * g_ref[...]

    pl.when(i == 0)(matmul)

    @pl.when(jnp.logical_and(i > 0, i < n_row_tiles))
    def _():
        finish()
        matmul()

    pl.when(i == n_row_tiles)(finish)


def _out_proj(mix_a, mix_b, mix_c, w_out_bf, layer, x2d, post_g):
    tokens = x2d.shape[0]
    tm, tn = OUT_TM, OUT_TN
    assert tokens % tm == 0 and D_MODEL % tn == 0
    n_row_tiles = tokens // tm
    n_col_tiles = D_MODEL // tn
    kernel = functools.partial(_out_proj_kernel, n_row_tiles=n_row_tiles, n_col_tiles=n_col_tiles)

    def lhs_row(i, j):
        return (jnp.minimum(i, n_row_tiles - 1), 0)

    def done_tile(i, j):
        return (jnp.maximum(i - 1, 0), jnp.where(i == 0, 0, j))

    return pl.pallas_call(
        kernel,
        grid=(n_row_tiles + 1, n_col_tiles),
        in_specs=[
            pl.BlockSpec((tm, A_Q), lhs_row),
            pl.BlockSpec((tm, B_W), lhs_row),
            pl.BlockSpec((tm, C_V), lhs_row),
            pl.BlockSpec((None, D_MIX, tn), lambda i, j: (layer, 0, j)),
            pl.BlockSpec((tm, tn), done_tile),
            pl.BlockSpec((1, tn), lambda i, j: (0, j)),
        ],
        out_specs=pl.BlockSpec((tm, tn), done_tile),
        out_shape=jax.ShapeDtypeStruct((tokens, D_MODEL), F32),
        scratch_shapes=[pltpu.VMEM((2, n_col_tiles, tm, tn), F32),
                        pltpu.VMEM((2, n_col_tiles, tm, 1), F32)],
        compiler_params=pltpu.CompilerParams(
            dimension_semantics=("arbitrary", "arbitrary"),
            vmem_limit_bytes=VMEM_LIMIT_BYTES),
        name="out_proj",
    )(mix_a, mix_b, mix_c, w_out_bf, x2d, post_g.reshape(1, D_MODEL))


def _angles(pos, dim, theta):
    inv = jnp.power(theta, -jnp.arange(0, dim, 2, dtype=F32) / dim)
    return pos.astype(F32)[:, None] * inv[None, :]


def _rope_tables(seq_len):
    t = jnp.arange(seq_len)
    ang_row = _angles(t // GRID_W, HEAD_DIM // 2, AXIAL_THETA)
    ang_col = _angles(t % GRID_W, HEAD_DIM // 2, AXIAL_THETA)
    cr, sr, cc_, sc_ = jnp.cos(ang_row), jnp.sin(ang_row), jnp.cos(ang_col), jnp.sin(ang_col)
    cos_ax = jnp.concatenate([cr, cr, cc_, cc_], axis=-1)
    sin_ax = jnp.concatenate([-sr, sr, -sc_, sc_], axis=-1)
    ang_t = _angles(t, ROPE_DIMS, ROPE_THETA)
    ct, st = jnp.cos(ang_t), jnp.sin(ang_t)
    rest = HEAD_DIM - ROPE_DIMS
    cos_p = jnp.concatenate([ct, ct, jnp.ones((seq_len, rest), F32)], axis=-1)
    sin_p = jnp.concatenate([-st, st, jnp.zeros((seq_len, rest), F32)], axis=-1)
    return cos_ax, sin_ax, cos_p, sin_p


def _trunk(x, params, w_in_bf, w_out_bf, bias_blocks):
    batch, seq_len, _ = x.shape
    tables = _rope_tables(seq_len)
    x2d = x.reshape(batch * seq_len, D_MODEL)
    depth = w_in_bf.shape[0]
    for l in range(depth):
        lam_init = 0.8 - 0.6 * math.exp(-0.3 * l)
        proj = _in_proj(x2d, params["pre_norm_g"][l], w_in_bf, l, params["a_q_norm_g"][l],
                        params["a_k_norm_g"][l], tables, seq_len)
        mix_a = _attn_a(proj, batch, seq_len)
        mix_b = _attn_b(proj, bias_blocks[l], batch, seq_len)
        mix_c = _attn_c(proj, params["c_lambda_q1"][l], params["c_lambda_k1"][l],
                        params["c_lambda_q2"][l], params["c_lambda_k2"][l],
                        params["c_subln_g"][l], lam_init, batch, seq_len)
        x2d = _out_proj(mix_a, mix_b, mix_c, w_out_bf, l, x2d, params["post_norm_g"][l])
    return x2d.reshape(batch, seq_len, D_MODEL)


def _prepare(params):
    depth = params["w_in"].shape[0]
    w_in_bf = params["w_in"].astype(BF16)
    w_out_bf = params["w_out"].astype(BF16)
    bias_blocks = [_bias_blocks(params["b_rel_bias"][l]) for l in range(depth)]
    return w_in_bf, w_out_bf, bias_blocks


def kernel(x_prompt, x_sample, pre_norm_g, post_norm_g, w_in, w_out, a_q_norm_g, a_k_norm_g,
           b_rel_bias, c_lambda_q1, c_lambda_k1, c_lambda_q2, c_lambda_k2, c_subln_g):
    params = dict(pre_norm_g=pre_norm_g, post_norm_g=post_norm_g, a_q_norm_g=a_q_norm_g,
                  a_k_norm_g=a_k_norm_g, c_lambda_q1=c_lambda_q1, c_lambda_k1=c_lambda_k1,
                  c_lambda_q2=c_lambda_q2, c_lambda_k2=c_lambda_k2, c_subln_g=c_subln_g)
    params.update(w_in=w_in, w_out=w_out, b_rel_bias=b_rel_bias)
    prepared = _prepare(params)
    y_prompt = _trunk(x_prompt, params, *prepared)
    y_sample = _trunk(x_sample, params, *prepared)
    return (y_prompt, y_sample)
```

```python
import functools
import math

import jax
import jax.numpy as jnp
import numpy as np
from jax import lax
from jax.experimental import pallas as pl
from jax.experimental.pallas import tpu as pltpu

F32 = jnp.float32
BF16 = jnp.bfloat16

D_MODEL = 4096
HEAD_DIM = 128
GRID_W = 64
A_HEADS = 12
A_KV_HEADS = 4
A_GROUP = A_HEADS // A_KV_HEADS
B_HEADS = 8
C_HEADS = 6
A_Q = A_HEADS * HEAD_DIM
A_KV = A_KV_HEADS * HEAD_DIM
B_W = B_HEADS * HEAD_DIM
C_QK = C_HEADS * 2 * HEAD_DIM
C_V = C_HEADS * 2 * HEAD_DIM
D_MIX = A_Q + B_W + C_V
D_IN = A_Q + 2 * A_KV + 3 * B_W + 2 * C_QK + C_V + D_MIX
NA_ROWS = 8
NA_COLS = 16
AXIAL_THETA = 10000.0
ROPE_THETA = 500000.0
ROPE_DIMS = HEAD_DIM // 4
EPS = 1e-6
NEG_INF = -1e30
SCALE = 1.0 / math.sqrt(HEAD_DIM)
SCALE_LOG2 = SCALE * math.log2(math.e)
ONES_ROWS = 16

OFF_QA = 0
OFF_KA = OFF_QA + A_Q
OFF_VA = OFF_KA + A_KV
OFF_QB = OFF_VA + A_KV
OFF_KB = OFF_QB + B_W
OFF_VB = OFF_KB + B_W
OFF_QC = OFF_VB + B_W
OFF_KC = OFF_QC + C_QK
OFF_VC = OFF_KC + C_QK
OFF_GATE = OFF_VC + C_V

VMEM_LIMIT_BYTES = 56 * 1024 * 1024

IN_TM = 512
IN_TN = 512
OUT_TM = 512
OUT_TN = 1024
A_TQ = 256
A_TK = 512
B_TQ = 2048
B_QROWS = 4
B_WROWS = B_QROWS + NA_ROWS
C_TQ = 512
C_TK = 512
KV_UNROLL = 8


def _swap_halves(x, h):
    lane = lax.broadcasted_iota(jnp.int32, x.shape, 1)
    first = (lane % (2 * h)) < h
    return jnp.where(first, pltpu.roll(x, HEAD_DIM - h, 1), pltpu.roll(x, h, 1))


def _silu(g):
    return g / (1.0 + jnp.exp(-g))


def _in_proj_kernel(x_ref, g_ref, w_ref, qn_ref, kn_ref, ca_ref, sa_ref, cc_ref, sc_ref,
                    o_ref, h_scr, acc_scr, *, n_tiles):
    j = pl.program_id(1)
    cur = j % 2
    prev = 1 - cur
    t = j - 1
    heads = IN_TN // HEAD_DIM

    @pl.when(j == 0)
    def _():
        x = x_ref[...]
        ms = jnp.sum(x * x, axis=-1, keepdims=True) * (1.0 / D_MODEL)
        h_scr[...] = ((x * lax.rsqrt(ms + EPS)) * g_ref[...]).astype(BF16)

    def matmul():
        acc_scr[cur] = jnp.dot(h_scr[...], w_ref[...], preferred_element_type=F32)

    def head_norm(y, g):
        ms = jnp.sum(y * y, axis=-1, keepdims=True) * (1.0 / HEAD_DIM)
        return (y * lax.rsqrt(ms + EPS)) * g

    def axial(y):
        return y * ca_ref[...] + _swap_halves(y, HEAD_DIM // 4) * sa_ref[...]

    def partial(y):
        return y * cc_ref[...] + _swap_halves(y, ROPE_DIMS // 2) * sc_ref[...]

    def finish(fn):
        for hh in range(heads):
            sl = slice(hh * HEAD_DIM, (hh + 1) * HEAD_DIM)
            o_ref[:, sl] = fn(acc_scr[prev, :, sl]).astype(BF16)

    t_qa = OFF_KA // IN_TN
    t_ka = OFF_VA // IN_TN
    t_qb0, t_qb1 = OFF_QB // IN_TN, OFF_KB // IN_TN
    t_qc0, t_kc0, t_vc0 = OFF_QC // IN_TN, OFF_KC // IN_TN, OFF_VC // IN_TN

    is_first = j == 0
    is_last = j == n_tiles
    is_qa = jnp.logical_and(t >= 0, t < t_qa)
    is_ka = jnp.logical_and(t >= t_qa, t < t_ka)
    is_qb = jnp.logical_and(t >= t_qb0, t < t_qb1)
    is_qc = jnp.logical_and(t >= t_qc0, t < t_kc0)
    is_kc = jnp.logical_and(t >= t_kc0, t < t_vc0)
    is_plain = jnp.logical_not(is_first | is_last | is_qa | is_ka | is_qb | is_qc | is_kc)

    def step(fn):
        finish(fn)
        matmul()

    pl.when(is_first)(matmul)
    pl.when(is_qa)(lambda: step(lambda y: axial(head_norm(y, qn_ref[...])) * SCALE_LOG2))
    pl.when(is_ka)(lambda: step(lambda y: axial(head_norm(y, kn_ref[...]))))
    pl.when(is_qb)(lambda: step(lambda y: y * SCALE_LOG2))
    pl.when(is_qc)(lambda: step(lambda y: partial(y) * SCALE_LOG2))
    pl.when(is_kc)(lambda: step(partial))
    pl.when(is_plain)(lambda: step(lambda y: y))
    pl.when(is_last)(lambda: finish(lambda y: y))


def _in_proj(x2d, pre_g, w_in_bf, layer, qn_g, kn_g, tables, seq_len):
    tokens = x2d.shape[0]
    tm = IN_TM
    assert tokens % tm == 0 and seq_len % tm == 0 and D_IN % IN_TN == 0
    for off in (OFF_KA, OFF_VA, OFF_QB, OFF_KB, OFF_QC, OFF_KC, OFF_VC):
        assert off % IN_TN == 0
    n_tiles = D_IN // IN_TN
    assert OFF_GATE <= (n_tiles - 1) * IN_TN
    pos_blocks = seq_len // tm
    tab_spec = pl.BlockSpec((tm, HEAD_DIM), lambda i, j: (i % pos_blocks, 0))
    vec_spec = pl.BlockSpec((1, HEAD_DIM), lambda i, j: (0, 0))
    return pl.pallas_call(
        functools.partial(_in_proj_kernel, n_tiles=n_tiles),
        grid=(tokens // tm, n_tiles + 1),
        in_specs=[
            pl.BlockSpec((tm, D_MODEL), lambda i, j: (i, 0)),
            pl.BlockSpec((1, D_MODEL), lambda i, j: (0, 0)),
            pl.BlockSpec((None, D_MODEL, IN_TN),
                         lambda i, j: (layer, 0, jnp.minimum(j, n_tiles - 1))),
            vec_spec, vec_spec, tab_spec, tab_spec, tab_spec, tab_spec,
        ],
        out_specs=pl.BlockSpec((tm, IN_TN), lambda i, j: (i, jnp.maximum(j - 1, 0))),
        out_shape=jax.ShapeDtypeStruct((tokens, D_IN), BF16),
        scratch_shapes=[pltpu.VMEM((tm, D_MODEL), BF16), pltpu.VMEM((2, tm, IN_TN), F32)],
        compiler_params=pltpu.CompilerParams(
            dimension_semantics=("parallel", "arbitrary"),
            vmem_limit_bytes=VMEM_LIMIT_BYTES),
        name="in_proj",
    )(x2d, pre_g.reshape(1, D_MODEL), w_in_bf, qn_g.reshape(1, HEAD_DIM),
      kn_g.reshape(1, HEAD_DIM), *tables)


def _build_vt(v_ref, vt_scr, *, n_kv, tk, dv):
    def chunk(c, carry):
        off = pl.multiple_of(c * tk, tk)
        vt_scr[c, 0:dv, :] = v_ref[pl.ds(off, tk), :].astype(F32).T.astype(BF16)
        row = lax.broadcasted_iota(jnp.int32, (ONES_ROWS, tk), 0)
        vt_scr[c, dv:dv + ONES_ROWS, :] = jnp.where(row == 0, 1.0, 0.0).astype(BF16)
        return carry

    lax.fori_loop(0, n_kv, chunk, 0)


def _kv_unroll(n_kv):
    unroll = min(KV_UNROLL, n_kv)
    assert unroll % 2 == 0 and n_kv % unroll == 0
    return unroll


def _flash_step(s, vt, m_scr, acc_scr):
    m_old = m_scr[...]
    m_new = jnp.maximum(m_old, jnp.max(s, axis=0, keepdims=True))
    alpha = jnp.exp2(m_old - m_new)
    p = jnp.exp2(s - m_new).astype(BF16)
    acc_scr[...] = alpha * acc_scr[...] + jnp.dot(vt, p, preferred_element_type=F32)
    m_scr[...] = m_new


def _attn_a_kernel(q_ref, qn_ref, k_ref, v_ref, g0_ref, g1_ref, g2_ref, o_ref,
                   qt_scr, vt_scr, s_scr, m_scr, acc_scr, *, tq, tk, n_kv):
    gate_refs = (g0_ref, g1_ref, g2_ref)
    i = pl.program_id(2)
    cur = i % 2
    assert n_kv % 2 == 0

    def load_qt(ref, slot):
        for g in range(A_GROUP):
            qt_scr[slot, :, g * tq:(g + 1) * tq] = (
                ref[:, g * HEAD_DIM:(g + 1) * HEAD_DIM].astype(F32).T.astype(BF16))

    def scores(j, slot):
        start = pl.multiple_of(j * tk, tk)
        return jnp.dot(k_ref[pl.ds(start, tk), :], qt_scr[slot], preferred_element_type=F32)

    @pl.when(i == 0)
    def _():
        _build_vt(v_ref, vt_scr, n_kv=n_kv, tk=tk, dv=HEAD_DIM)
        load_qt(q_ref, 0)
        s_scr[0] = scores(0, 0)

    m_scr[...] = jnp.full(m_scr.shape, NEG_INF, F32)
    acc_scr[...] = jnp.zeros(acc_scr.shape, F32)

    unroll = _kv_unroll(n_kv)

    def step_group(jj, is_tail):
        for u in range(unroll):
            j = unroll * jj + u
            slot = u % 2
            if is_tail and u == unroll - 1:
                load_qt(qn_ref, 1 - cur)
                s_scr[0] = scores(0, 1 - cur)
            else:
                s_scr[1 - slot] = scores(j + 1, cur)
            _flash_step(s_scr[slot], vt_scr[j], m_scr, acc_scr)

    def body(jj, carry):
        step_group(jj, False)
        return carry

    lax.fori_loop(0, n_kv // unroll - 1, body, 0)
    step_group(n_kv // unroll - 1, True)
    acc = acc_scr[...]
    out = (acc[0:HEAD_DIM] / acc[HEAD_DIM:HEAD_DIM + 1]).T
    for g in range(A_GROUP):
        sl = slice(g * HEAD_DIM, (g + 1) * HEAD_DIM)
        gate = gate_refs[g][...].astype(F32)
        o_ref[:, sl] = (out[g * tq:(g + 1) * tq, :] * _silu(gate)).astype(BF16)


def _attn_a(proj, batch, seq_len):
    tq, tk = A_TQ, A_TK
    assert seq_len % tq == 0 and seq_len % tk == 0
    nq = seq_len // tq
    n_kv = seq_len // tk
    gw = A_GROUP * HEAD_DIM
    kernel = functools.partial(_attn_a_kernel, tq=tq, tk=tk, n_kv=n_kv)

    def gate_spec(g):
        return pl.BlockSpec(
            (tq, HEAD_DIM),
            lambda b, h, i: (b * nq + i, OFF_GATE // HEAD_DIM + A_GROUP * h + g))

    return pl.pallas_call(
        kernel,
        grid=(batch, A_KV_HEADS, nq),
        in_specs=[
            pl.BlockSpec((tq, gw), lambda b, h, i: (b * nq + i, h)),
            pl.BlockSpec((tq, gw), lambda b, h, i: (b * nq + jnp.minimum(i + 1, nq - 1), h)),
            pl.BlockSpec((seq_len, HEAD_DIM), lambda b, h, i: (b, OFF_KA // HEAD_DIM + h)),
            pl.BlockSpec((seq_len, HEAD_DIM), lambda b, h, i: (b, OFF_VA // HEAD_DIM + h)),
            gate_spec(0), gate_spec(1), gate_spec(2),
        ],
        out_specs=pl.BlockSpec((tq, gw), lambda b, h, i: (b * nq + i, h)),
        out_shape=jax.ShapeDtypeStruct((batch * seq_len, A_Q), BF16),
        scratch_shapes=[
            pltpu.VMEM((2, HEAD_DIM, A_GROUP * tq), BF16),
            pltpu.VMEM((n_kv, HEAD_DIM + ONES_ROWS, tk), BF16),
            pltpu.VMEM((2, tk, A_GROUP * tq), F32),
            pltpu.VMEM((1, A_GROUP * tq), F32),
            pltpu.VMEM((HEAD_DIM + ONES_ROWS, A_GROUP * tq), F32),
        ],
        compiler_params=pltpu.CompilerParams(
            dimension_semantics=("parallel", "parallel", "arbitrary"),
            vmem_limit_bytes=VMEM_LIMIT_BYTES),
        name="attn_a",
    )(proj, proj, proj, proj, proj, proj, proj)


def _attn_b_kernel(q_ref, k_ref, v_ref, gate_ref, bias_ref, o_ref, vt_scr,
                   *, blocks_per_step, grid_rows):
    t = pl.program_id(2)
    bq = B_QROWS * GRID_W
    wk = B_WROWS * GRID_W
    n_blocks = grid_rows // B_QROWS

    @pl.when(t == 0)
    def _():
        _build_vt(v_ref, vt_scr, n_kv=n_blocks, tk=bq, dv=HEAD_DIM)

    def block_rows(u):
        return slice(u * bq, (u + 1) * bq)

    def window(u):
        blk = t * blocks_per_step + u
        win_row = jnp.clip(blk * B_QROWS - NA_ROWS // 2, 0, grid_rows - B_WROWS)
        case = jnp.where(blk == 0, 0, jnp.where(blk == n_blocks - 1, 2, 1))
        return win_row, case

    def scores(u):
        win_row, case = window(u)
        qt = q_ref[block_rows(u), :].astype(F32).T.astype(BF16)
        k = k_ref[pl.ds(pl.multiple_of(win_row * GRID_W, bq), wk), :]
        return jnp.dot(k, qt, preferred_element_type=F32) + bias_ref[case]

    def attend(u, s):
        chunk0 = window(u)[0] // B_QROWS
        p = jnp.exp2(s - jnp.max(s, axis=0, keepdims=True)).astype(BF16)
        vt = jnp.concatenate([vt_scr[chunk0 + w] for w in range(B_WROWS // B_QROWS)], axis=1)
        return jnp.dot(vt, p, preferred_element_type=F32)

    def store(u, acc):
        o = (acc[0:HEAD_DIM] / acc[HEAD_DIM:HEAD_DIM + 1]).T
        gate = gate_ref[block_rows(u), :].astype(F32)
        o_ref[block_rows(u), :] = (o * _silu(gate)).astype(BF16)

    s_next = scores(0)
    acc_prev = None
    for u in range(blocks_per_step):
        s_cur = s_next
        if u + 1 < blocks_per_step:
            s_next = scores(u + 1)
        acc = attend(u, s_cur)
        if acc_prev is not None:
            store(u - 1, acc_prev)
        acc_prev = acc
    store(blocks_per_step - 1, acc_prev)


def _attn_b(proj, bias_blocks, batch, seq_len):
    tq = min(B_TQ, seq_len)
    bq = B_QROWS * GRID_W
    assert seq_len % tq == 0 and tq % bq == 0
    grid_rows = seq_len // GRID_W
    assert grid_rows % B_QROWS == 0 and grid_rows >= 4 * B_QROWS
    nq = seq_len // tq
    n_blocks = grid_rows // B_QROWS
    kernel = functools.partial(_attn_b_kernel, blocks_per_step=tq // bq, grid_rows=grid_rows)
    return pl.pallas_call(
        kernel,
        grid=(batch, B_HEADS, nq),
        in_specs=[
            pl.BlockSpec((tq, HEAD_DIM), lambda b, h, i: (b * nq + i, OFF_QB // HEAD_DIM + h)),
            pl.BlockSpec((seq_len, HEAD_DIM), lambda b, h, i: (b, OFF_KB // HEAD_DIM + h)),
            pl.BlockSpec((seq_len, HEAD_DIM), lambda b, h, i: (b, OFF_VB // HEAD_DIM + h)),
            pl.BlockSpec((tq, HEAD_DIM),
                         lambda b, h, i: (b * nq + i, (OFF_GATE + A_Q) // HEAD_DIM + h)),
            pl.BlockSpec((None, 3, B_WROWS * GRID_W, bq), lambda b, h, i: (h, 0, 0, 0)),
        ],
        out_specs=pl.BlockSpec((tq, HEAD_DIM), lambda b, h, i: (b * nq + i, h)),
        out_shape=jax.ShapeDtypeStruct((batch * seq_len, B_W), BF16),
        scratch_shapes=[pltpu.VMEM((n_blocks, HEAD_DIM + ONES_ROWS, bq), BF16)],
        compiler_params=pltpu.CompilerParams(
            dimension_semantics=("parallel", "parallel", "arbitrary"),
            vmem_limit_bytes=VMEM_LIMIT_BYTES),
        name="attn_b",
    )(proj, proj, proj, proj, bias_blocks)


def _bias_blocks(rel_bias):
    heads = rel_bias.shape[0]
    n_dr, n_dc = 2 * NA_ROWS - 1, 2 * NA_COLS - 1
    c = np.arange(GRID_W)
    col_start = np.clip(c - NA_COLS // 2, 0, GRID_W - NA_COLS)
    col_ok = (c[None, :] >= col_start[:, None]) & (c[None, :] < col_start[:, None] + NA_COLS)
    dc = np.clip(c[None, :] - c[:, None], -(NA_COLS - 1), NA_COLS - 1) + (NA_COLS - 1)
    pick_dc = (dc.reshape(-1, 1) == np.arange(n_dc)[None, :]).astype(np.float32)
    any_rows = 8 * B_QROWS
    tables = []
    for first_row, win_row in ((0, 0), (2 * B_QROWS, B_QROWS), (any_rows - B_QROWS,
                                                                 any_rows - B_WROWS)):
        r = first_row + np.arange(B_QROWS)
        kr = win_row + np.arange(B_WROWS)
        band0 = np.clip(r - NA_ROWS // 2, 0, any_rows - NA_ROWS)
        row_ok = (kr[None, :] >= band0[:, None]) & (kr[None, :] < band0[:, None] + NA_ROWS)
        dr = np.clip(kr[None, :] - r[:, None] + (NA_ROWS - 1), 0, n_dr - 1)
        pick_dr = (dr.reshape(-1, 1) == np.arange(n_dr)[None, :]).astype(np.float32)
        bias = jnp.einsum("hij,pi,qj->hpq", rel_bias.astype(F32), pick_dr, pick_dc,
                          precision=lax.Precision.HIGHEST)
        bias = bias.reshape(heads, B_QROWS, B_WROWS, GRID_W, GRID_W)
        ok = row_ok[:, :, None, None] & col_ok[None, None, :, :]
        bias = jnp.where(ok[None], bias * math.log2(math.e), NEG_INF)
        tables.append(jnp.transpose(bias, (0, 2, 4, 1, 3)).reshape(
            heads, B_WROWS * GRID_W, B_QROWS * GRID_W))
    return jnp.stack(tables, axis=1)


def _attn_c_kernel(q_ref, qn_ref, k_ref, v_ref, gate_ref, lq1_ref, lk1_ref, lq2_ref, lk2_ref,
                   sg_ref, o_ref, qt_scr, vt_scr, s_scr, m1_scr, a1_scr, m2_scr, a2_scr,
                   *, tk, n_kv, lam_init):
    dv = 2 * HEAD_DIM
    streams = ((m1_scr, a1_scr), (m2_scr, a2_scr))
    i = pl.program_id(2)
    cur = i % 2
    assert n_kv % 2 == 0

    def load_qt(ref, slot, u):
        qt_scr[slot, u] = ref[:, u * HEAD_DIM:(u + 1) * HEAD_DIM].astype(F32).T.astype(BF16)

    def scores(j, slot, u):
        start = pl.multiple_of(j * tk, tk)
        k = k_ref[pl.ds(start, tk), u * HEAD_DIM:(u + 1) * HEAD_DIM]
        return jnp.dot(k, qt_scr[slot, u], preferred_element_type=F32)

    @pl.when(i == 0)
    def _():
        _build_vt(v_ref, vt_scr, n_kv=n_kv, tk=tk, dv=dv)
        for u in range(2):
            load_qt(q_ref, 0, u)
            s_scr[0, u] = scores(0, 0, u)

    for m_scr, a_scr in streams:
        m_scr[...] = jnp.full(m_scr.shape, NEG_INF, F32)
        a_scr[...] = jnp.zeros(a_scr.shape, F32)

    unroll = _kv_unroll(n_kv)

    def step_group(jj, is_tail):
        for t in range(unroll):
            j = unroll * jj + t
            slot = t % 2
            for u, (m_scr, a_scr) in enumerate(streams):
                if is_tail and t == unroll - 1:
                    load_qt(qn_ref, 1 - cur, u)
                    s_scr[0, u] = scores(0, 1 - cur, u)
                else:
                    s_scr[1 - slot, u] = scores(j + 1, cur, u)
                _flash_step(s_scr[slot, u], vt_scr[j], m_scr, a_scr)

    def body(jj, carry):
        step_group(jj, False)
        return carry

    lax.fori_loop(0, n_kv // unroll - 1, body, 0)
    step_group(n_kv // unroll - 1, True)

    lam = (jnp.exp(jnp.sum(lq1_ref[...] * lk1_ref[...], axis=-1, keepdims=True))
           - jnp.exp(jnp.sum(lq2_ref[...] * lk2_ref[...], axis=-1, keepdims=True))
           + lam_init)
    a1 = a1_scr[...]
    a2 = a2_scr[...]
    o = (a1[0:dv] / a1[dv:dv + 1] - lam * (a2[0:dv] / a2[dv:dv + 1])).T
    ms = jnp.sum(o * o, axis=-1, keepdims=True) * (1.0 / (2 * HEAD_DIM))
    o = (o * lax.rsqrt(ms + EPS)) * sg_ref[...] * (1.0 - lam_init)
    o_ref[...] = (o * _silu(gate_ref[...].astype(F32))).astype(BF16)


def _attn_c(proj, lq1, lk1, lq2, lk2, subln_g, lam_init, batch, seq_len):
    tq, tk = C_TQ, C_TK
    assert seq_len % tq == 0 and seq_len % tk == 0
    nq = seq_len // tq
    n_kv = seq_len // tk
    hw = 2 * HEAD_DIM
    kernel = functools.partial(_attn_c_kernel, tk=tk, n_kv=n_kv, lam_init=lam_init)
    vec = pl.BlockSpec((1, HEAD_DIM), lambda b, h, i: (0, 0))
    return pl.pallas_call(
        kernel,
        grid=(batch, C_HEADS, nq),
        in_specs=[
            pl.BlockSpec((tq, hw), lambda b, h, i: (b * nq + i, OFF_QC // hw + h)),
            pl.BlockSpec((tq, hw),
                         lambda b, h, i: (b * nq + jnp.minimum(i + 1, nq - 1), OFF_QC // hw + h)),
            pl.BlockSpec((seq_len, hw), lambda b, h, i: (b, OFF_KC // hw + h)),
            pl.BlockSpec((seq_len, hw), lambda b, h, i: (b, OFF_VC // hw + h)),
            pl.BlockSpec((tq, hw), lambda b, h, i: (b * nq + i, (OFF_GATE + A_Q + B_W) // hw + h)),
            vec, vec, vec, vec,
            pl.BlockSpec((1, hw), lambda b, h, i: (0, 0)),
        ],
        out_specs=pl.BlockSpec((tq, hw), lambda b, h, i: (b * nq + i, h)),
        out_shape=jax.ShapeDtypeStruct((batch * seq_len, C_V), BF16),
        scratch_shapes=[
            pltpu.VMEM((2, 2, HEAD_DIM, tq), BF16),
            pltpu.VMEM((n_kv, hw + ONES_ROWS, tk), BF16),
            pltpu.VMEM((2, 2, tk, tq), F32),
            pltpu.VMEM((1, tq), F32), pltpu.VMEM((hw + ONES_ROWS, tq), F32),
            pltpu.VMEM((1, tq), F32), pltpu.VMEM((hw + ONES_ROWS, tq), F32),
        ],
        compiler_params=pltpu.CompilerParams(
            dimension_semantics=("parallel", "parallel", "arbitrary"),
            vmem_limit_bytes=VMEM_LIMIT_BYTES),
        name="attn_c",
    )(proj, proj, proj, proj, proj, lq1.reshape(1, HEAD_DIM), lk1.reshape(1, HEAD_DIM),
      lq2.reshape(1, HEAD_DIM), lk2.reshape(1, HEAD_DIM), subln_g.reshape(1, hw))


def _out_proj_kernel(a_ref, b_ref, c_ref, w_ref, x_ref, g_ref, o_ref,
                     y_scr, ss_scr, *, n_row_tiles, n_col_tiles):
    i = pl.program_id(0)
    j = pl.program_id(1)
    cur = i % 2
    prev = 1 - cur

    def matmul():
        y = jnp.dot(a_ref[...], w_ref[0:A_Q, :], preferred_element_type=F32)
        y = y + jnp.dot(b_ref[...], w_ref[A_Q:A_Q + B_W, :], preferred_element_type=F32)
        y = y + jnp.dot(c_ref[...], w_ref[A_Q + B_W:D_MIX, :], preferred_element_type=F32)
        y_scr[cur, j] = y
        ss_scr[cur, j] = jnp.sum(y * y, axis=-1, keepdims=True)

    def finish():
        ss = ss_scr[prev, 0]
        for jj in range(1, n_col_tiles):
            ss = ss + ss_scr[prev, jj]
        inv = lax.rsqrt(ss * (1.0 / D_MODEL) + EPS)
        o_ref[...] = x_ref[...] + (y_scr[prev, j] * inv) * g_ref[...]

    pl.when(i == 0)(matmul)

    @pl.when(jnp.logical_and(i > 0, i < n_row_tiles))
    def _():
        finish()
        matmul()

    pl.when(i == n_row_tiles)(finish)


def _out_proj(mix_a, mix_b, mix_c, w_out_bf, layer, x2d, post_g):
    tokens = x2d.shape[0]
    tm, tn = OUT_TM, OUT_TN
    assert tokens % tm == 0 and D_MODEL % tn == 0
    n_row_tiles = tokens // tm
    n_col_tiles = D_MODEL // tn
    kernel = functools.partial(_out_proj_kernel, n_row_tiles=n_row_tiles, n_col_tiles=n_col_tiles)

    def lhs_row(i, j):
        return (jnp.minimum(i, n_row_tiles - 1), 0)

    def done_tile(i, j):
        return (jnp.maximum(i - 1, 0), jnp.where(i == 0, 0, j))

    return pl.pallas_call(
        kernel,
        grid=(n_row_tiles + 1, n_col_tiles),
        in_specs=[
            pl.BlockSpec((tm, A_Q), lhs_row),
            pl.BlockSpec((tm, B_W), lhs_row),
            pl.BlockSpec((tm, C_V), lhs_row),
            pl.BlockSpec((None, D_MIX, tn), lambda i, j: (layer, 0, j)),
            pl.BlockSpec((tm, tn), done_tile),
            pl.BlockSpec((1, tn), lambda i, j: (0, j)),
        ],
        out_specs=pl.BlockSpec((tm, tn), done_tile),
        out_shape=jax.ShapeDtypeStruct((tokens, D_MODEL), F32),
        scratch_shapes=[pltpu.VMEM((2, n_col_tiles, tm, tn), F32),
                        pltpu.VMEM((2, n_col_tiles, tm, 1), F32)],
        compiler_params=pltpu.CompilerParams(
            dimension_semantics=("arbitrary", "arbitrary"),
            vmem_limit_bytes=VMEM_LIMIT_BYTES),
        name="out_proj",
    )(mix_a, mix_b, mix_c, w_out_bf, x2d, post_g.reshape(1, D_MODEL))


def _angles(pos, dim, theta):
    inv = jnp.power(theta, -jnp.arange(0, dim, 2, dtype=F32) / dim)
    return pos.astype(F32)[:, None] * inv[None, :]


def _rope_tables(seq_len):
    t = jnp.arange(seq_len)
    ang_row = _angles(t // GRID_W, HEAD_DIM // 2, AXIAL_THETA)
    ang_col = _angles(t % GRID_W, HEAD_DIM // 2, AXIAL_THETA)
    cr, sr, cc_, sc_ = jnp.cos(ang_row), jnp.sin(ang_row), jnp.cos(ang_col), jnp.sin(ang_col)
    cos_ax = jnp.concatenate([cr, cr, cc_, cc_], axis=-1)
    sin_ax = jnp.concatenate([-sr, sr, -sc_, sc_], axis=-1)
    ang_t = _angles(t, ROPE_DIMS, ROPE_THETA)
    ct, st = jnp.cos(ang_t), jnp.sin(ang_t)
    rest = HEAD_DIM - ROPE_DIMS
    cos_p = jnp.concatenate([ct, ct, jnp.ones((seq_len, rest), F32)], axis=-1)
    sin_p = jnp.concatenate([-st, st, jnp.zeros((seq_len, rest), F32)], axis=-1)
    return cos_ax, sin_ax, cos_p, sin_p


def _trunk(x, params, w_in_bf, w_out_bf, bias_blocks):
    batch, seq_len, _ = x.shape
    tables = _rope_tables(seq_len)
    x2d = x.reshape(batch * seq_len, D_MODEL)
    depth = w_in_bf.shape[0]
    for l in range(depth):
        lam_init = 0.8 - 0.6 * math.exp(-0.3 * l)
        proj = _in_proj(x2d, params["pre_norm_g"][l], w_in_bf, l, params["a_q_norm_g"][l],
                        params["a_k_norm_g"][l], tables, seq_len)
        mix_a = _attn_a(proj, batch, seq_len)
        mix_b = _attn_b(proj, bias_blocks[l], batch, seq_len)
        mix_c = _attn_c(proj, params["c_lambda_q1"][l], params["c_lambda_k1"][l],
                        params["c_lambda_q2"][l], params["c_lambda_k2"][l],
                        params["c_subln_g"][l], lam_init, batch, seq_len)
        x2d = _out_proj(mix_a, mix_b, mix_c, w_out_bf, l, x2d, params["post_norm_g"][l])
    return x2d.reshape(batch, seq_len, D_MODEL)


def _prepare(params):
    depth = params["w_in"].shape[0]
    w_in_bf = params["w_in"].astype(BF16)
    w_out_bf = params["w_out"].astype(BF16)
    bias_blocks = [_bias_blocks(params["b_rel_bias"][l]) for l in range(depth)]
    return w_in_bf, w_out_bf, bias_blocks


def kernel(x_prompt, x_sample, pre_norm_g, post_norm_g, w_in, w_out, a_q_norm_g, a_k_norm_g,
           b_rel_bias, c_lambda_q1, c_lambda_k1, c_lambda_q2, c_lambda_k2, c_subln_g):
    params = dict(pre_norm_g=pre_norm_g, post_norm_g=post_norm_g, a_q_norm_g=a_q_norm_g,
                  a_k_norm_g=a_k_norm_g, c_lambda_q1=c_lambda_q1, c_lambda_k1=c_lambda_k1,
                  c_lambda_q2=c_lambda_q2, c_lambda_k2=c_lambda_k2, c_subln_g=c_subln_g)
    params.update(w_in=w_in, w_out=w_out, b_rel_bias=b_rel_bias)
    prepared = _prepare(params)
    y_prompt = _trunk(x_prompt, params, *prepared)
    y_sample = _trunk(x_sample, params, *prepared)
    return (y_prompt, y_sample)
```

```python
import functools
import math

import jax
import jax.numpy as jnp
import numpy as np
from jax import lax
from jax.experimental import pallas as pl
from jax.experimental.pallas import tpu as pltpu

F32 = jnp.float32
BF16 = jnp.bfloat16

D_MODEL = 4096
HEAD_DIM = 128
GRID_W = 64
A_HEADS = 12
A_KV_HEADS = 4
A_GROUP = A_HEADS // A_KV_HEADS
B_HEADS = 8
C_HEADS = 6
A_Q = A_HEADS * HEAD_DIM
A_KV = A_KV_HEADS * HEAD_DIM
B_W = B_HEADS * HEAD_DIM
C_QK = C_HEADS * 2 * HEAD_DIM
C_V = C_HEADS * 2 * HEAD_DIM
D_MIX = A_Q + B_W + C_V
D_IN = A_Q + 2 * A_KV + 3 * B_W + 2 * C_QK + C_V + D_MIX
NA_ROWS = 8
NA_COLS = 16
AXIAL_THETA = 10000.0
ROPE_THETA = 500000.0
ROPE_DIMS = HEAD_DIM // 4
EPS = 1e-6
NEG_INF = -1e30
SCALE = 1.0 / math.sqrt(HEAD_DIM)
SCALE_LOG2 = SCALE * math.log2(math.e)
ONES_ROWS = 16

OFF_QA = 0
OFF_KA = OFF_QA + A_Q
OFF_VA = OFF_KA + A_KV
OFF_QB = OFF_VA + A_KV
OFF_KB = OFF_QB + B_W
OFF_VB = OFF_KB + B_W
OFF_QC = OFF_VB + B_W
OFF_KC = OFF_QC + C_QK
OFF_VC = OFF_KC + C_QK
OFF_GATE = OFF_VC + C_V

VMEM_LIMIT_BYTES = 56 * 1024 * 1024

IN_TM = 512
IN_TN = 512
IN_SEG = 512
OUT_TM = 512
OUT_TN = 1024
A_TQ = 256
A_STREAMS = 2
A_TK = 512
B_TQ = 2048
B_QROWS = 4
B_WROWS = B_QROWS + NA_ROWS
C_TQ = 512
C_TK = 512
KV_UNROLL = 16


def _swap_halves(x, h):
    lane = lax.broadcasted_iota(jnp.int32, x.shape, 1)
    first = (lane % (2 * h)) < h
    return jnp.where(first, pltpu.roll(x, HEAD_DIM - h, 1), pltpu.roll(x, h, 1))


def _silu(g):
    return g / (1.0 + jnp.exp(-g))


def _segment_kind(off):
    if off < OFF_KA:
        return "qa"
    if off < OFF_VA:
        return "ka"
    if OFF_QB <= off < OFF_KB:
        return "qb"
    if OFF_QC <= off < OFF_KC:
        return "qc"
    if OFF_KC <= off < OFF_VC:
        return "kc"
    return "plain"


def _in_tile_kinds(n_tiles):
    segs = IN_TN // IN_SEG
    return [tuple(_segment_kind((t * segs + s) * IN_SEG) for s in range(segs))
            for t in range(n_tiles)]


def _in_proj_kernel(x_ref, g_ref, w_ref, qn_ref, kn_ref, ca_ref, sa_ref, cc_ref, sc_ref,
                    o_ref, h_scr, acc_scr, *, n_tiles):
    j = pl.program_id(1)
    cur = j % 2
    prev = 1 - cur
    heads = IN_SEG // HEAD_DIM

    @pl.when(j == 0)
    def _():
        x = x_ref[...]
        ms = jnp.sum(x * x, axis=-1, keepdims=True) * (1.0 / D_MODEL)
        h_scr[...] = ((x * lax.rsqrt(ms + EPS)) * g_ref[...]).astype(BF16)

    def matmul():
        acc_scr[cur] = jnp.dot(h_scr[...], w_ref[...], preferred_element_type=F32)

    def head_norm(y, g):
        ms = jnp.sum(y * y, axis=-1, keepdims=True) * (1.0 / HEAD_DIM)
        return (y * lax.rsqrt(ms + EPS)) * g

    def axial(y):
        return y * ca_ref[...] + _swap_halves(y, HEAD_DIM // 4) * sa_ref[...]

    def partial(y):
        return y * cc_ref[...] + _swap_halves(y, ROPE_DIMS // 2) * sc_ref[...]

    epilogues = {
        "qa": lambda y: axial(head_norm(y, qn_ref[...])) * SCALE_LOG2,
        "ka": lambda y: axial(head_norm(y, kn_ref[...])),
        "qb": lambda y: y * SCALE_LOG2,
        "qc": lambda y: partial(y) * SCALE_LOG2,
        "kc": partial,
        "plain": lambda y: y,
    }

    def finish(kinds):
        for seg, kind in enumerate(kinds):
            for hh in range(heads):
                lo = seg * IN_SEG + hh * HEAD_DIM
                o_ref[:, lo:lo + HEAD_DIM] = epilogues[kind](
                    acc_scr[prev, :, lo:lo + HEAD_DIM]).astype(BF16)

    def step(kinds):
        finish(kinds)
        matmul()

    tile_kinds = _in_tile_kinds(n_tiles)
    pl.when(j == 0)(matmul)
    for kinds in sorted(set(tile_kinds[:-1])):
        tiles = [t for t in range(n_tiles - 1) if tile_kinds[t] == kinds]
        cond = functools.reduce(jnp.logical_or, [j == t + 1 for t in tiles])
        pl.when(cond)(functools.partial(step, kinds))
    pl.when(j == n_tiles)(functools.partial(finish, tile_kinds[-1]))


def _in_proj(x2d, pre_g, w_in_bf, layer, qn_g, kn_g, tables, seq_len):
    tokens = x2d.shape[0]
    tm = IN_TM
    assert tokens % tm == 0 and seq_len % tm == 0 and D_IN % IN_TN == 0 and IN_TN % IN_SEG == 0
    for off in (OFF_KA, OFF_VA, OFF_QB, OFF_KB, OFF_QC, OFF_KC, OFF_VC):
        assert off % IN_SEG == 0
    n_tiles = D_IN // IN_TN
    pos_blocks = seq_len // tm
    tab_spec = pl.BlockSpec((tm, HEAD_DIM), lambda i, j: (i % pos_blocks, 0))
    vec_spec = pl.BlockSpec((1, HEAD_DIM), lambda i, j: (0, 0))
    return pl.pallas_call(
        functools.partial(_in_proj_kernel, n_tiles=n_tiles),
        grid=(tokens // tm, n_tiles + 1),
        in_specs=[
            pl.BlockSpec((tm, D_MODEL), lambda i, j: (i, 0)),
            pl.BlockSpec((1, D_MODEL), lambda i, j: (0, 0)),
            pl.BlockSpec((None, D_MODEL, IN_TN),
                         lambda i, j: (layer, 0, jnp.minimum(j, n_tiles - 1))),
            vec_spec, vec_spec, tab_spec, tab_spec, tab_spec, tab_spec,
        ],
        out_specs=pl.BlockSpec((tm, IN_TN), lambda i, j: (i, jnp.maximum(j - 1, 0))),
        out_shape=jax.ShapeDtypeStruct((tokens, D_IN), BF16),
        scratch_shapes=[pltpu.VMEM((tm, D_MODEL), BF16), pltpu.VMEM((2, tm, IN_TN), F32)],
        compiler_params=pltpu.CompilerParams(
            dimension_semantics=("parallel", "arbitrary"),
            vmem_limit_bytes=VMEM_LIMIT_BYTES),
        name="in_proj",
    )(x2d, pre_g.reshape(1, D_MODEL), w_in_bf, qn_g.reshape(1, HEAD_DIM),
      kn_g.reshape(1, HEAD_DIM), *tables)


def _build_vt(v_ref, vt_scr, *, n_kv, tk, dv):
    def chunk(c, carry):
        off = pl.multiple_of(c * tk, tk)
        vt_scr[c, 0:dv, :] = v_ref[pl.ds(off, tk), :].astype(F32).T.astype(BF16)
        row = lax.broadcasted_iota(jnp.int32, (ONES_ROWS, tk), 0)
        vt_scr[c, dv:dv + ONES_ROWS, :] = jnp.where(row == 0, 1.0, 0.0).astype(BF16)
        return carry

    lax.fori_loop(0, n_kv, chunk, 0)


def _kv_unroll(n_kv, cap):
    unroll = min(cap, n_kv)
    assert unroll % 2 == 0 and n_kv % unroll == 0
    return unroll


def _flash_step(s, vt, m_scr, acc_scr):
    m_old = m_scr[...]
    m_new = jnp.maximum(m_old, jnp.max(s, axis=0, keepdims=True))
    alpha = jnp.exp2(m_old - m_new)
    p = jnp.exp2(s - m_new).astype(BF16)
    acc_scr[...] = alpha * acc_scr[...] + jnp.dot(vt, p, preferred_element_type=F32)
    m_scr[...] = m_new


def _attn_a_kernel(q_ref, qn_ref, k_ref, v_ref, g0_ref, g1_ref, g2_ref, o_ref,
                   qt_scr, vt_scr, s_scr, m_scr, acc_scr, *, tq, tk, n_kv):
    gate_refs = (g0_ref, g1_ref, g2_ref)
    i = pl.program_id(2)
    cur = i % 2
    assert n_kv % 2 == 0
    streams = range(A_STREAMS)

    def load_qt(ref, slot, st):
        for g in range(A_GROUP):
            qt_scr[slot, st, :, g * tq:(g + 1) * tq] = (
                ref[st * tq:(st + 1) * tq,
                    g * HEAD_DIM:(g + 1) * HEAD_DIM].astype(F32).T.astype(BF16))

    def scores(j, slot, st):
        start = pl.multiple_of(j * tk, tk)
        return jnp.dot(k_ref[pl.ds(start, tk), :], qt_scr[slot, st],
                       preferred_element_type=F32)

    @pl.when(i == 0)
    def _():
        _build_vt(v_ref, vt_scr, n_kv=n_kv, tk=tk, dv=HEAD_DIM)
        for st in streams:
            load_qt(q_ref, 0, st)
            s_scr[0, st] = scores(0, 0, st)

    m_scr[...] = jnp.full(m_scr.shape, NEG_INF, F32)
    acc_scr[...] = jnp.zeros(acc_scr.shape, F32)

    unroll = _kv_unroll(n_kv, KV_UNROLL // A_STREAMS)

    def step_group(jj, is_tail):
        for u in range(unroll):
            j = unroll * jj + u
            slot = u % 2
            for st in streams:
                if is_tail and u == unroll - 1:
                    load_qt(qn_ref, 1 - cur, st)
                    s_scr[0, st] = scores(0, 1 - cur, st)
                else:
                    s_scr[1 - slot, st] = scores(j + 1, cur, st)
                _flash_step(s_scr[slot, st], vt_scr[j], m_scr.at[st], acc_scr.at[st])

    def body(jj, carry):
        step_group(jj, False)
        return carry

    lax.fori_loop(0, n_kv // unroll - 1, body, 0)
    step_group(n_kv // unroll - 1, True)
    for st in streams:
        acc = acc_scr[st]
        out = (acc[0:HEAD_DIM] / acc[HEAD_DIM:HEAD_DIM + 1]).T
        rows = slice(st * tq, (st + 1) * tq)
        for g in range(A_GROUP):
            sl = slice(g * HEAD_DIM, (g + 1) * HEAD_DIM)
            gate = gate_refs[g][rows, :].astype(F32)
            o_ref[rows, sl] = (out[g * tq:(g + 1) * tq, :] * _silu(gate)).astype(BF16)


def _attn_a(proj, batch, seq_len):
    tq, tk = A_TQ, A_TK
    tile = A_STREAMS * tq
    assert seq_len % tile == 0 and seq_len % tk == 0
    nq = seq_len // tile
    n_kv = seq_len // tk
    gw = A_GROUP * HEAD_DIM
    kernel = functools.partial(_attn_a_kernel, tq=tq, tk=tk, n_kv=n_kv)

    def gate_spec(g):
        return pl.BlockSpec(
            (tile, HEAD_DIM),
            lambda b, h, i: (b * nq + i, OFF_GATE // HEAD_DIM + A_GROUP * h + g))

    return pl.pallas_call(
        kernel,
        grid=(batch, A_KV_HEADS, nq),
        in_specs=[
            pl.BlockSpec((tile, gw), lambda b, h, i: (b * nq + i, h)),
            pl.BlockSpec((tile, gw), lambda b, h, i: (b * nq + jnp.minimum(i + 1, nq - 1), h)),
            pl.BlockSpec((seq_len, HEAD_DIM), lambda b, h, i: (b, OFF_KA // HEAD_DIM + h)),
            pl.BlockSpec((seq_len, HEAD_DIM), lambda b, h, i: (b, OFF_VA // HEAD_DIM + h)),
            gate_spec(0), gate_spec(1), gate_spec(2),
        ],
        out_specs=pl.BlockSpec((tile, gw), lambda b, h, i: (b * nq + i, h)),
        out_shape=jax.ShapeDtypeStruct((batch * seq_len, A_Q), BF16),
        scratch_shapes=[
            pltpu.VMEM((2, A_STREAMS, HEAD_DIM, A_GROUP * tq), BF16),
            pltpu.VMEM((n_kv, HEAD_DIM + ONES_ROWS, tk), BF16),
            pltpu.VMEM((2, A_STREAMS, tk, A_GROUP * tq), F32),
            pltpu.VMEM((A_STREAMS, 1, A_GROUP * tq), F32),
            pltpu.VMEM((A_STREAMS, HEAD_DIM + ONES_ROWS, A_GROUP * tq), F32),
        ],
        compiler_params=pltpu.CompilerParams(
            dimension_semantics=("parallel", "parallel", "arbitrary"),
            vmem_limit_bytes=VMEM_LIMIT_BYTES),
        name="attn_a",
    )(proj, proj, proj, proj, proj, proj, proj)


def _attn_b_kernel(q_ref, k_ref, v_ref, gate_ref, bias_ref, o_ref, vt_scr,
                   *, blocks_per_step, grid_rows):
    t = pl.program_id(2)
    bq = B_QROWS * GRID_W
    wk = B_WROWS * GRID_W
    n_blocks = grid_rows // B_QROWS

    @pl.when(t == 0)
    def _():
        _build_vt(v_ref, vt_scr, n_kv=n_blocks, tk=bq, dv=HEAD_DIM)

    def block_rows(u):
        return slice(u * bq, (u + 1) * bq)

    def window(u):
        blk = t * blocks_per_step + u
        win_row = jnp.clip(blk * B_QROWS - NA_ROWS // 2, 0, grid_rows - B_WROWS)
        case = jnp.where(blk == 0, 0, jnp.where(blk == n_blocks - 1, 2, 1))
        return win_row, case

    def scores(u):
        win_row, case = window(u)
        qt = q_ref[block_rows(u), :].astype(F32).T.astype(BF16)
        k = k_ref[pl.ds(pl.multiple_of(win_row * GRID_W, bq), wk), :]
        return jnp.dot(k, qt, preferred_element_type=F32) + bias_ref[case]

    def attend(u, s):
        chunk0 = window(u)[0] // B_QROWS
        p = jnp.exp2(s - jnp.max(s, axis=0, keepdims=True)).astype(BF16)
        vt = jnp.concatenate([vt_scr[chunk0 + w] for w in range(B_WROWS // B_QROWS)], axis=1)
        return jnp.dot(vt, p, preferred_element_type=F32)

    def store(u, acc):
        o = (acc[0:HEAD_DIM] / acc[HEAD_DIM:HEAD_DIM + 1]).T
        gate = gate_ref[block_rows(u), :].astype(F32)
        o_ref[block_rows(u), :] = (o * _silu(gate)).astype(BF16)

    s_next = scores(0)
    acc_prev = None
    for u in range(blocks_per_step):
        s_cur = s_next
        if u + 1 < blocks_per_step:
            s_next = scores(u + 1)
        acc = attend(u, s_cur)
        if acc_prev is not None:
            store(u - 1, acc_prev)
        acc_prev = acc
    store(blocks_per_step - 1, acc_prev)


def _attn_b(proj, bias_blocks, batch, seq_len):
    tq = min(B_TQ, seq_len)
    bq = B_QROWS * GRID_W
    assert seq_len % tq == 0 and tq % bq == 0
    grid_rows = seq_len // GRID_W
    assert grid_rows % B_QROWS == 0 and grid_rows >= 4 * B_QROWS
    nq = seq_len // tq
    n_blocks = grid_rows // B_QROWS
    kernel = functools.partial(_attn_b_kernel, blocks_per_step=tq // bq, grid_rows=grid_rows)
    return pl.pallas_call(
        kernel,
        grid=(batch, B_HEADS, nq),
        in_specs=[
            pl.BlockSpec((tq, HEAD_DIM), lambda b, h, i: (b * nq + i, OFF_QB // HEAD_DIM + h)),
            pl.BlockSpec((seq_len, HEAD_DIM), lambda b, h, i: (b, OFF_KB // HEAD_DIM + h)),
            pl.BlockSpec((seq_len, HEAD_DIM), lambda b, h, i: (b, OFF_VB // HEAD_DIM + h)),
            pl.BlockSpec((tq, HEAD_DIM),
                         lambda b, h, i: (b * nq + i, (OFF_GATE + A_Q) // HEAD_DIM + h)),
            pl.BlockSpec((None, 3, B_WROWS * GRID_W, bq), lambda b, h, i: (h, 0, 0, 0)),
        ],
        out_specs=pl.BlockSpec((tq, HEAD_DIM), lambda b, h, i: (b * nq + i, h)),
        out_shape=jax.ShapeDtypeStruct((batch * seq_len, B_W), BF16),
        scratch_shapes=[pltpu.VMEM((n_blocks, HEAD_DIM + ONES_ROWS, bq), BF16)],
        compiler_params=pltpu.CompilerParams(
            dimension_semantics=("parallel", "parallel", "arbitrary"),
            vmem_limit_bytes=VMEM_LIMIT_BYTES),
        name="attn_b",
    )(proj, proj, proj, proj, bias_blocks)


def _bias_blocks(rel_bias):
    heads = rel_bias.shape[0]
    n_dr, n_dc = 2 * NA_ROWS - 1, 2 * NA_COLS - 1
    c = np.arange(GRID_W)
    col_start = np.clip(c - NA_COLS // 2, 0, GRID_W - NA_COLS)
    col_ok = (c[None, :] >= col_start[:, None]) & (c[None, :] < col_start[:, None] + NA_COLS)
    dc = np.clip(c[None, :] - c[:, None], -(NA_COLS - 1), NA_COLS - 1) + (NA_COLS - 1)
    pick_dc = (dc.reshape(-1, 1) == np.arange(n_dc)[None, :]).astype(np.float32)
    any_rows = 8 * B_QROWS
    tables = []
    for first_row, win_row in ((0, 0), (2 * B_QROWS, B_QROWS), (any_rows - B_QROWS,
                                                                 any_rows - B_WROWS)):
        r = first_row + np.arange(B_QROWS)
        kr = win_row + np.arange(B_WROWS)
        band0 = np.clip(r - NA_ROWS // 2, 0, any_rows - NA_ROWS)
        row_ok = (kr[None, :] >= band0[:, None]) & (kr[None, :] < band0[:, None] + NA_ROWS)
        dr = np.clip(kr[None, :] - r[:, None] + (NA_ROWS - 1), 0, n_dr - 1)
        pick_dr = (dr.reshape(-1, 1) == np.arange(n_dr)[None, :]).astype(np.float32)
        bias = jnp.einsum("hij,pi,qj->hpq", rel_bias.astype(F32), pick_dr, pick_dc,
                          precision=lax.Precision.HIGHEST)
        bias = bias.reshape(heads, B_QROWS, B_WROWS, GRID_W, GRID_W)
        ok = row_ok[:, :, None, None] & col_ok[None, None, :, :]
        bias = jnp.where(ok[None], bias * math.log2(math.e), NEG_INF)
        tables.append(jnp.transpose(bias, (0, 2, 4, 1, 3)).reshape(
            heads, B_WROWS * GRID_W, B_QROWS * GRID_W))
    return jnp.stack(tables, axis=1)


def _attn_c_kernel(q_ref, qn_ref, k_ref, v_ref, gate_ref, lq1_ref, lk1_ref, lq2_ref, lk2_ref,
                   sg_ref, o_ref, qt_scr, vt_scr, s_scr, m1_scr, a1_scr, m2_scr, a2_scr,
                   *, tk, n_kv, lam_init):
    dv = 2 * HEAD_DIM
    streams = ((m1_scr, a1_scr), (m2_scr, a2_scr))
    i = pl.program_id(2)
    cur = i % 2
    assert n_kv % 2 == 0

    def load_qt(ref, slot, u):
        qt_scr[slot, u] = ref[:, u * HEAD_DIM:(u + 1) * HEAD_DIM].astype(F32).T.astype(BF16)

    def scores(j, slot, u):
        start = pl.multiple_of(j * tk, tk)
        k = k_ref[pl.ds(start, tk), u * HEAD_DIM:(u + 1) * HEAD_DIM]
        return jnp.dot(k, qt_scr[slot, u], preferred_element_type=F32)

    @pl.when(i == 0)
    def _():
        _build_vt(v_ref, vt_scr, n_kv=n_kv, tk=tk, dv=dv)
        for u in range(2):
            load_qt(q_ref, 0, u)
            s_scr[0, u] = scores(0, 0, u)

    for m_scr, a_scr in streams:
        m_scr[...] = jnp.full(m_scr.shape, NEG_INF, F32)
        a_scr[...] = jnp.zeros(a_scr.shape, F32)

    unroll = _kv_unroll(n_kv, KV_UNROLL // 2)

    def step_group(jj, is_tail):
        for t in range(unroll):
            j = unroll * jj + t
            slot = t % 2
            for u, (m_scr, a_scr) in enumerate(streams):
                if is_tail and t == unroll - 1:
                    load_qt(qn_ref, 1 - cur, u)
                    s_scr[0, u] = scores(0, 1 - cur, u)
                else:
                    s_scr[1 - slot, u] = scores(j + 1, cur, u)
                _flash_step(s_scr[slot, u], vt_scr[j], m_scr, a_scr)

    def body(jj, carry):
        step_group(jj, False)
        return carry

    lax.fori_loop(0, n_kv // unroll - 1, body, 0)
    step_group(n_kv // unroll - 1, True)

    lam = (jnp.exp(jnp.sum(lq1_ref[...] * lk1_ref[...], axis=-1, keepdims=True))
           - jnp.exp(jnp.sum(lq2_ref[...] * lk2_ref[...], axis=-1, keepdims=True))
           + lam_init)
    a1 = a1_scr[...]
    a2 = a2_scr[...]
    o = (a1[0:dv] / a1[dv:dv + 1] - lam * (a2[0:dv] / a2[dv:dv + 1])).T
    ms = jnp.sum(o * o, axis=-1, keepdims=True) * (1.0 / (2 * HEAD_DIM))
    o = (o * lax.rsqrt(ms + EPS)) * sg_ref[...] * (1.0 - lam_init)
    o_ref[...] = (o * _silu(gate_ref[...].astype(F32))).astype(BF16)


def _attn_c(proj, lq1, lk1, lq2, lk2, subln_g, lam_init, batch, seq_len):
    tq, tk = C_TQ, C_TK
    assert seq_len % tq == 0 and seq_len % tk == 0
    nq = seq_len // tq
    n_kv = seq_len // tk
    hw = 2 * HEAD_DIM
    kernel = functools.partial(_attn_c_kernel, tk=tk, n_kv=n_kv, lam_init=lam_init)
    vec = pl.BlockSpec((1, HEAD_DIM), lambda b, h, i: (0, 0))
    return pl.pallas_call(
        kernel,
        grid=(batch, C_HEADS, nq),
        in_specs=[
            pl.BlockSpec((tq, hw), lambda b, h, i: (b * nq + i, OFF_QC // hw + h)),
            pl.BlockSpec((tq, hw),
                         lambda b, h, i: (b * nq + jnp.minimum(i + 1, nq - 1), OFF_QC // hw + h)),
            pl.BlockSpec((seq_len, hw), lambda b, h, i: (b, OFF_KC // hw + h)),
            pl.BlockSpec((seq_len, hw), lambda b, h, i: (b, OFF_VC // hw + h)),
            pl.BlockSpec((tq, hw), lambda b, h, i: (b * nq + i, (OFF_GATE + A_Q + B_W) // hw + h)),
            vec, vec, vec, vec,
            pl.BlockSpec((1, hw), lambda b, h, i: (0, 0)),
        ],
        out_specs=pl.BlockSpec((tq, hw), lambda b, h, i: (b * nq + i, h)),
        out_shape=jax.ShapeDtypeStruct((batch * seq_len, C_V), BF16),
        scratch_shapes=[
            pltpu.VMEM((2, 2, HEAD_DIM, tq), BF16),
            pltpu.VMEM((n_kv, hw + ONES_ROWS, tk), BF16),
            pltpu.VMEM((2, 2, tk, tq), F32),
            pltpu.VMEM((1, tq), F32), pltpu.VMEM((hw + ONES_ROWS, tq), F32),
            pltpu.VMEM((1, tq), F32), pltpu.VMEM((hw + ONES_ROWS, tq), F32),
        ],
        compiler_params=pltpu.CompilerParams(
            dimension_semantics=("parallel", "parallel", "arbitrary"),
            vmem_limit_bytes=VMEM_LIMIT_BYTES),
        name="attn_c",
    )(proj, proj, proj, proj, proj, lq1.reshape(1, HEAD_DIM), lk1.reshape(1, HEAD_DIM),
      lq2.reshape(1, HEAD_DIM), lk2.reshape(1, HEAD_DIM), subln_g.reshape(1, hw))


def _out_proj_kernel(a_ref, b_ref, c_ref, w_ref, x_ref, g_ref, o_ref,
                     y_scr, ss_scr, *, n_row_tiles, n_col_tiles):
    i = pl.program_id(0)
    j = pl.program_id(1)
    cur = i % 2
    prev = 1 - cur

    def matmul():
        y = jnp.dot(a_ref[...], w_ref[0:A_Q, :], preferred_element_type=F32)
        y = y + jnp.dot(b_ref[...], w_ref[A_Q:A_Q + B_W, :], preferred_element_type=F32)
        y = y + jnp.dot(c_ref[...], w_ref[A_Q + B_W:D_MIX, :], preferred_element_type=F32)
        y_scr[cur, j] = y
        ss_scr[cur, j] = jnp.sum(y * y, axis=-1, keepdims=True)

    def finish():
        ss = ss_scr[prev, 0]
        for jj in range(1, n_col_tiles):
            ss = ss + ss_scr[prev, jj]
        inv = lax.rsqrt(ss * (1.0 / D_MODEL) + EPS)
        o_ref[...] = x_ref[...] + (y_scr[prev, j] * inv) * g_ref[...]

    pl.when(i == 0)(matmul)

    @pl.when(jnp.logical_and(i > 0, i < n_row_tiles))
    def _():
        finish()
        matmul()

    pl.when(i == n_row_tiles)(finish)


def _out_proj(mix_a, mix_b, mix_c, w_out_bf, layer, x2d, post_g):
    tokens = x2d.shape[0]
    tm, tn = OUT_TM, OUT_TN
    assert tokens % tm == 0 and D_MODEL % tn == 0
    n_row_tiles = tokens // tm
    n_col_tiles = D_MODEL // tn
    kernel = functools.partial(_out_proj_kernel, n_row_tiles=n_row_tiles, n_col_tiles=n_col_tiles)

    def lhs_row(i, j):
        return (jnp.minimum(i, n_row_tiles - 1), 0)

    def done_tile(i, j):
        return (jnp.maximum(i - 1, 0), jnp.where(i == 0, 0, j))

    return pl.pallas_call(
        kernel,
        grid=(n_row_tiles + 1, n_col_tiles),
        in_specs=[
            pl.BlockSpec((tm, A_Q), lhs_row),
            pl.BlockSpec((tm, B_W), lhs_row),
            pl.BlockSpec((tm, C_V), lhs_row),
            pl.BlockSpec((None, D_MIX, tn), lambda i, j: (layer, 0, j)),
            pl.BlockSpec((tm, tn), done_tile),
            pl.BlockSpec((1, tn), lambda i, j: (0, j)),
        ],
        out_specs=pl.BlockSpec((tm, tn), done_tile),
        out_shape=jax.ShapeDtypeStruct((tokens, D_MODEL), F32),
        scratch_shapes=[pltpu.VMEM((2, n_col_tiles, tm, tn), F32),
                        pltpu.VMEM((2, n_col_tiles, tm, 1), F32)],
        compiler_params=pltpu.CompilerParams(
            dimension_semantics=("arbitrary", "arbitrary"),
            vmem_limit_bytes=VMEM_LIMIT_BYTES),
        name="out_proj",
    )(mix_a, mix_b, mix_c, w_out_bf, x2d, post_g.reshape(1, D_MODEL))


def _angles(pos, dim, theta):
    inv = jnp.power(theta, -jnp.arange(0, dim, 2, dtype=F32) / dim)
    return pos.astype(F32)[:, None] * inv[None, :]


def _rope_tables(seq_len):
    t = jnp.arange(seq_len)
    ang_row = _angles(t // GRID_W, HEAD_DIM // 2, AXIAL_THETA)
    ang_col = _angles(t % GRID_W, HEAD_DIM // 2, AXIAL_THETA)
    cr, sr, cc_, sc_ = jnp.cos(ang_row), jnp.sin(ang_row), jnp.cos(ang_col), jnp.sin(ang_col)
    cos_ax = jnp.concatenate([cr, cr, cc_, cc_], axis=-1)
    sin_ax = jnp.concatenate([-sr, sr, -sc_, sc_], axis=-1)
    ang_t = _angles(t, ROPE_DIMS, ROPE_THETA)
    ct, st = jnp.cos(ang_t), jnp.sin(ang_t)
    rest = HEAD_DIM - ROPE_DIMS
    cos_p = jnp.concatenate([ct, ct, jnp.ones((seq_len, rest), F32)], axis=-1)
    sin_p = jnp.concatenate([-st, st, jnp.zeros((seq_len, rest), F32)], axis=-1)
    return cos_ax, sin_ax, cos_p, sin_p


def _trunk(x, params, w_in_bf, w_out_bf, bias_blocks):
    batch, seq_len, _ = x.shape
    tables = _rope_tables(seq_len)
    x2d = x.reshape(batch * seq_len, D_MODEL)
    depth = w_in_bf.shape[0]
    for l in range(depth):
        lam_init = 0.8 - 0.6 * math.exp(-0.3 * l)
        proj = _in_proj(x2d, params["pre_norm_g"][l], w_in_bf, l, params["a_q_norm_g"][l],
                        params["a_k_norm_g"][l], tables, seq_len)
        mix_a = _attn_a(proj, batch, seq_len)
        mix_b = _attn_b(proj, bias_blocks[l], batch, seq_len)
        mix_c = _attn_c(proj, params["c_lambda_q1"][l], params["c_lambda_k1"][l],
                        params["c_lambda_q2"][l], params["c_lambda_k2"][l],
                        params["c_subln_g"][l], lam_init, batch, seq_len)
        x2d = _out_proj(mix_a, mix_b, mix_c, w_out_bf, l, x2d, params["post_norm_g"][l])
    return x2d.reshape(batch, seq_len, D_MODEL)


def _prepare(params):
    depth = params["w_in"].shape[0]
    w_in_bf = params["w_in"].astype(BF16)
    w_out_bf = params["w_out"].astype(BF16)
    bias_blocks = [_bias_blocks(params["b_rel_bias"][l]) for l in range(depth)]
    return w_in_bf, w_out_bf, bias_blocks


def kernel(x_prompt, x_sample, pre_norm_g, post_norm_g, w_in, w_out, a_q_norm_g, a_k_norm_g,
           b_rel_bias, c_lambda_q1, c_lambda_k1, c_lambda_q2, c_lambda_k2, c_subln_g):
    params = dict(pre_norm_g=pre_norm_g, post_norm_g=post_norm_g, a_q_norm_g=a_q_norm_g,
                  a_k_norm_g=a_k_norm_g, c_lambda_q1=c_lambda_q1, c_lambda_k1=c_lambda_k1,
                  c_lambda_q2=c_lambda_q2, c_lambda_k2=c_lambda_k2, c_subln_g=c_subln_g)
    params.update(w_in=w_in, w_out=w_out, b_rel_bias=b_rel_bias)
    prepared = _prepare(params)
    y_prompt = _trunk(x_prompt, params, *prepared)
    y_sample = _trunk(x_sample, params, *prepared)
    return (y_prompt, y_sample)
```

```python
import functools
import math

import jax
import jax.numpy as jnp
import numpy as np
from jax import lax
from jax.experimental import pallas as pl
from jax.experimental.pallas import tpu as pltpu

F32 = jnp.float32
BF16 = jnp.bfloat16

D_MODEL = 4096
HEAD_DIM = 128
GRID_W = 64
A_HEADS = 12
A_KV_HEADS = 4
A_GROUP = A_HEADS // A_KV_HEADS
B_HEADS = 8
C_HEADS = 6
A_Q = A_HEADS * HEAD_DIM
A_KV = A_KV_HEADS * HEAD_DIM
B_W = B_HEADS * HEAD_DIM
C_QK = C_HEADS * 2 * HEAD_DIM
C_V = C_HEADS * 2 * HEAD_DIM
D_MIX = A_Q + B_W + C_V
D_IN = A_Q + 2 * A_KV + 3 * B_W + 2 * C_QK + C_V + D_MIX
NA_ROWS = 8
NA_COLS = 16
AXIAL_THETA = 10000.0
ROPE_THETA = 500000.0
ROPE_DIMS = HEAD_DIM // 4
EPS = 1e-6
NEG_INF = -1e30
SCALE = 1.0 / math.sqrt(HEAD_DIM)
SCALE_LOG2 = SCALE * math.log2(math.e)
ONES_ROWS = 16

OFF_QA = 0
OFF_KA = OFF_QA + A_Q
OFF_VA = OFF_KA + A_KV
OFF_QB = OFF_VA + A_KV
OFF_KB = OFF_QB + B_W
OFF_VB = OFF_KB + B_W
OFF_QC = OFF_VB + B_W
OFF_KC = OFF_QC + C_QK
OFF_VC = OFF_KC + C_QK
OFF_GATE = OFF_VC + C_V

VMEM_LIMIT_BYTES = 56 * 1024 * 1024

IN_TM = 512
IN_TN = 512
IN_SEG = 512
OUT_TM = 512
OUT_TN = 1024
A_TQ = 256
A_STREAMS = 2
A_TK = 512
B_TQ = 2048
B_QROWS = 4
B_WROWS = B_QROWS + NA_ROWS
B_CASES = 3
C_TQ = 512
C_TK = 512
KV_UNROLL = 16


def _swap_halves(x, h):
    lane = lax.broadcasted_iota(jnp.int32, x.shape, 1)
    first = (lane % (2 * h)) < h
    return jnp.where(first, pltpu.roll(x, HEAD_DIM - h, 1), pltpu.roll(x, h, 1))


def _silu(g):
    return g / (1.0 + jnp.exp(-g))


def _segment_kind(off):
    if off < OFF_KA:
        return "qa"
    if off < OFF_VA:
        return "ka"
    if OFF_QB <= off < OFF_KB:
        return "qb"
    if OFF_QC <= off < OFF_KC:
        return "qc"
    if OFF_KC <= off < OFF_VC:
        return "kc"
    return "plain"


def _in_tile_kinds(n_tiles):
    segs = IN_TN // IN_SEG
    return [tuple(_segment_kind((t * segs + s) * IN_SEG) for s in range(segs))
            for t in range(n_tiles)]


def _in_proj_kernel(x_ref, g_ref, w_ref, qn_ref, kn_ref, ca_ref, sa_ref, cc_ref, sc_ref,
                    o_ref, h_scr, acc_scr, *, n_tiles):
    j = pl.program_id(1)
    cur = j % 2
    prev = 1 - cur
    heads = IN_SEG // HEAD_DIM

    def pre_norm():
        x = x_ref[...]
        ms = jnp.sum(x * x, axis=-1, keepdims=True) * (1.0 / D_MODEL)
        h_scr[...] = ((x * lax.rsqrt(ms + EPS)) * g_ref[...]).astype(BF16)

    def matmul():
        acc_scr[cur] = jnp.dot(h_scr[...], w_ref[...], preferred_element_type=F32)

    def first_step():
        pre_norm()
        matmul()

    def head_norm(y, g):
        ms = jnp.sum(y * y, axis=-1, keepdims=True) * (1.0 / HEAD_DIM)
        return (y * lax.rsqrt(ms + EPS)) * g

    def axial(y):
        return y * ca_ref[...] + _swap_halves(y, HEAD_DIM // 4) * sa_ref[...]

    def partial(y):
        return y * cc_ref[...] + _swap_halves(y, ROPE_DIMS // 2) * sc_ref[...]

    epilogues = {
        "qa": lambda y: axial(head_norm(y, qn_ref[...])) * SCALE_LOG2,
        "ka": lambda y: axial(head_norm(y, kn_ref[...])),
        "qb": lambda y: y * SCALE_LOG2,
        "qc": lambda y: partial(y) * SCALE_LOG2,
        "kc": partial,
        "plain": lambda y: y,
    }

    def finish(kinds):
        for seg, kind in enumerate(kinds):
            for hh in range(heads):
                lo = seg * IN_SEG + hh * HEAD_DIM
                o_ref[:, lo:lo + HEAD_DIM] = epilogues[kind](
                    acc_scr[prev, :, lo:lo + HEAD_DIM]).astype(BF16)

    def step(kinds):
        finish(kinds)
        matmul()

    tile_kinds = _in_tile_kinds(n_tiles)
    pl.when(j == 0)(first_step)
    for kinds in sorted(set(tile_kinds[:-1])):
        tiles = [t for t in range(n_tiles - 1) if tile_kinds[t] == kinds]
        cond = functools.reduce(jnp.logical_or, [j == t + 1 for t in tiles])
        pl.when(cond)(functools.partial(step, kinds))
    pl.when(j == n_tiles)(functools.partial(finish, tile_kinds[-1]))


def _in_proj(x2d, pre_g, w_in_bf, layer, qn_g, kn_g, tables, seq_len):
    tokens = x2d.shape[0]
    tm = IN_TM
    assert tokens % tm == 0 and seq_len % tm == 0 and D_IN % IN_TN == 0 and IN_TN % IN_SEG == 0
    for off in (OFF_KA, OFF_VA, OFF_QB, OFF_KB, OFF_QC, OFF_KC, OFF_VC):
        assert off % IN_SEG == 0
    n_tiles = D_IN // IN_TN
    pos_blocks = seq_len // tm
    tab_spec = pl.BlockSpec((tm, HEAD_DIM), lambda i, j: (i % pos_blocks, 0))
    vec_spec = pl.BlockSpec((1, HEAD_DIM), lambda i, j: (0, 0))
    return pl.pallas_call(
        functools.partial(_in_proj_kernel, n_tiles=n_tiles),
        grid=(tokens // tm, n_tiles + 1),
        in_specs=[
            pl.BlockSpec((tm, D_MODEL), lambda i, j: (i, 0)),
            pl.BlockSpec((1, D_MODEL), lambda i, j: (0, 0)),
            pl.BlockSpec((None, D_MODEL, IN_TN),
                         lambda i, j: (layer, 0, jnp.minimum(j, n_tiles - 1))),
            vec_spec, vec_spec, tab_spec, tab_spec, tab_spec, tab_spec,
        ],
        out_specs=pl.BlockSpec((tm, IN_TN), lambda i, j: (i, jnp.maximum(j - 1, 0))),
        out_shape=jax.ShapeDtypeStruct((tokens, D_IN), BF16),
        scratch_shapes=[pltpu.VMEM((tm, D_MODEL), BF16), pltpu.VMEM((2, tm, IN_TN), F32)],
        compiler_params=pltpu.CompilerParams(
            dimension_semantics=("parallel", "arbitrary"),
            vmem_limit_bytes=VMEM_LIMIT_BYTES),
        name="in_proj",
    )(x2d, pre_g.reshape(1, D_MODEL), w_in_bf, qn_g.reshape(1, HEAD_DIM),
      kn_g.reshape(1, HEAD_DIM), *tables)


def _build_vt(v_ref, vt_scr, *, n_kv, tk, dv):
    def chunk(c, carry):
        off = pl.multiple_of(c * tk, tk)
        vt_scr[c, 0:dv, :] = v_ref[pl.ds(off, tk), :].astype(F32).T.astype(BF16)
        row = lax.broadcasted_iota(jnp.int32, (ONES_ROWS, tk), 0)
        vt_scr[c, dv:dv + ONES_ROWS, :] = jnp.where(row == 0, 1.0, 0.0).astype(BF16)
        return carry

    lax.fori_loop(0, n_kv, chunk, 0)


def _kv_unroll(n_kv, cap):
    unroll = min(cap, n_kv)
    assert unroll % 2 == 0 and n_kv % unroll == 0
    return unroll


def _flash_step(s, vt, m_scr, acc_scr):
    m_old = m_scr[...]
    m_new = jnp.maximum(m_old, jnp.max(s, axis=0, keepdims=True))
    alpha = jnp.exp2(m_old - m_new)
    p = jnp.exp2(s - m_new).astype(BF16)
    acc_scr[...] = alpha * acc_scr[...] + jnp.dot(vt, p, preferred_element_type=F32)
    m_scr[...] = m_new


def _attn_a_kernel(q_ref, qn_ref, k_ref, v_ref, g0_ref, g1_ref, g2_ref, o_ref,
                   qt_scr, vt_scr, s_scr, m_scr, acc_scr, *, tq, tk, n_kv):
    gate_refs = (g0_ref, g1_ref, g2_ref)
    i = pl.program_id(2)
    cur = i % 2
    assert n_kv % 2 == 0
    streams = range(A_STREAMS)

    def load_qt(ref, slot, st):
        for g in range(A_GROUP):
            qt_scr[slot, st, :, g * tq:(g + 1) * tq] = (
                ref[st * tq:(st + 1) * tq,
                    g * HEAD_DIM:(g + 1) * HEAD_DIM].astype(F32).T.astype(BF16))

    def scores(j, slot, st):
        start = pl.multiple_of(j * tk, tk)
        return jnp.dot(k_ref[pl.ds(start, tk), :], qt_scr[slot, st],
                       preferred_element_type=F32)

    @pl.when(i == 0)
    def _():
        _build_vt(v_ref, vt_scr, n_kv=n_kv, tk=tk, dv=HEAD_DIM)
        for st in streams:
            load_qt(q_ref, 0, st)
            s_scr[0, st] = scores(0, 0, st)

    m_scr[...] = jnp.full(m_scr.shape, NEG_INF, F32)
    acc_scr[...] = jnp.zeros(acc_scr.shape, F32)

    unroll = _kv_unroll(n_kv, KV_UNROLL // A_STREAMS)

    def step_group(jj, is_tail):
        for u in range(unroll):
            j = unroll * jj + u
            slot = u % 2
            for st in streams:
                if is_tail and u == unroll - 1:
                    load_qt(qn_ref, 1 - cur, st)
                    s_scr[0, st] = scores(0, 1 - cur, st)
                else:
                    s_scr[1 - slot, st] = scores(j + 1, cur, st)
                _flash_step(s_scr[slot, st], vt_scr[j], m_scr.at[st], acc_scr.at[st])

    def body(jj, carry):
        step_group(jj, False)
        return carry

    lax.fori_loop(0, n_kv // unroll - 1, body, 0)
    step_group(n_kv // unroll - 1, True)
    for st in streams:
        acc = acc_scr[st]
        out = (acc[0:HEAD_DIM] / acc[HEAD_DIM:HEAD_DIM + 1]).T
        rows = slice(st * tq, (st + 1) * tq)
        for g in range(A_GROUP):
            sl = slice(g * HEAD_DIM, (g + 1) * HEAD_DIM)
            gate = gate_refs[g][rows, :].astype(F32)
            o_ref[rows, sl] = (out[g * tq:(g + 1) * tq, :] * _silu(gate)).astype(BF16)


def _attn_a(proj, batch, seq_len):
    tq, tk = A_TQ, A_TK
    tile = A_STREAMS * tq
    assert seq_len % tile == 0 and seq_len % tk == 0
    nq = seq_len // tile
    n_kv = seq_len // tk
    gw = A_GROUP * HEAD_DIM
    kernel = functools.partial(_attn_a_kernel, tq=tq, tk=tk, n_kv=n_kv)

    def gate_spec(g):
        return pl.BlockSpec(
            (tile, HEAD_DIM),
            lambda b, h, i: (b * nq + i, OFF_GATE // HEAD_DIM + A_GROUP * h + g))

    return pl.pallas_call(
        kernel,
        grid=(batch, A_KV_HEADS, nq),
        in_specs=[
            pl.BlockSpec((tile, gw), lambda b, h, i: (b * nq + i, h)),
            pl.BlockSpec((tile, gw), lambda b, h, i: (b * nq + jnp.minimum(i + 1, nq - 1), h)),
            pl.BlockSpec((seq_len, HEAD_DIM), lambda b, h, i: (b, OFF_KA // HEAD_DIM + h)),
            pl.BlockSpec((seq_len, HEAD_DIM), lambda b, h, i: (b, OFF_VA // HEAD_DIM + h)),
            gate_spec(0), gate_spec(1), gate_spec(2),
        ],
        out_specs=pl.BlockSpec((tile, gw), lambda b, h, i: (b * nq + i, h)),
        out_shape=jax.ShapeDtypeStruct((batch * seq_len, A_Q), BF16),
        scratch_shapes=[
            pltpu.VMEM((2, A_STREAMS, HEAD_DIM, A_GROUP * tq), BF16),
            pltpu.VMEM((n_kv, HEAD_DIM + ONES_ROWS, tk), BF16),
            pltpu.VMEM((2, A_STREAMS, tk, A_GROUP * tq), F32),
            pltpu.VMEM((A_STREAMS, 1, A_GROUP * tq), F32),
            pltpu.VMEM((A_STREAMS, HEAD_DIM + ONES_ROWS, A_GROUP * tq), F32),
        ],
        compiler_params=pltpu.CompilerParams(
            dimension_semantics=("parallel", "parallel", "arbitrary"),
            vmem_limit_bytes=VMEM_LIMIT_BYTES),
        name="attn_a",
    )(proj, proj, proj, proj, proj, proj, proj)


def _attn_b_kernel(q_ref, k_ref, v_ref, gate_ref, bias_ref, o_ref, vt_scr,
                   *, blocks_per_step, grid_rows):
    t = pl.program_id(2)
    bq = B_QROWS * GRID_W
    wk = B_WROWS * GRID_W
    n_blocks = grid_rows // B_QROWS

    @pl.when(t == 0)
    def _():
        _build_vt(v_ref, vt_scr, n_kv=n_blocks, tk=bq, dv=HEAD_DIM)

    def block_rows(u):
        return slice(u * bq, (u + 1) * bq)

    def window(u):
        blk = t * blocks_per_step + u
        win_row = jnp.clip(blk * B_QROWS - NA_ROWS // 2, 0, grid_rows - B_WROWS)
        case = jnp.where(blk == 0, 0, jnp.where(blk == n_blocks - 1, 2, 1))
        return win_row, case

    def scores(u):
        win_row, case = window(u)
        qt = q_ref[block_rows(u), :].astype(F32).T.astype(BF16)
        k = k_ref[pl.ds(pl.multiple_of(win_row * GRID_W, bq), wk), :]
        return jnp.dot(k, qt, preferred_element_type=F32) + bias_ref[case]

    def softmax(s):
        return jnp.exp2(s - jnp.max(s, axis=0, keepdims=True)).astype(BF16)

    def weighted_values(u, p):
        chunk0 = window(u)[0] // B_QROWS
        vt = jnp.concatenate([vt_scr[chunk0 + w] for w in range(B_WROWS // B_QROWS)], axis=1)
        return jnp.dot(vt, p, preferred_element_type=F32)

    def store(u, acc):
        o = (acc[0:HEAD_DIM] / acc[HEAD_DIM:HEAD_DIM + 1]).T
        gate = gate_ref[block_rows(u), :].astype(F32)
        o_ref[block_rows(u), :] = (o * _silu(gate)).astype(BF16)

    n = blocks_per_step
    s, p, acc = {0: scores(0)}, {}, {}
    for u in range(n + 2):
        if u + 1 < n:
            s[u + 1] = scores(u + 1)
        if 0 <= u - 1 < n:
            acc[u - 1] = weighted_values(u - 1, p.pop(u - 1))
        if u < n:
            p[u] = softmax(s.pop(u))
        if 0 <= u - 2 < n:
            store(u - 2, acc.pop(u - 2))


def _attn_b(proj, bias_blocks, layer, batch, seq_len):
    tq = min(B_TQ, seq_len)
    bq = B_QROWS * GRID_W
    assert seq_len % tq == 0 and tq % bq == 0
    grid_rows = seq_len // GRID_W
    assert grid_rows % B_QROWS == 0 and grid_rows >= 4 * B_QROWS
    nq = seq_len // tq
    n_blocks = grid_rows // B_QROWS
    kernel = functools.partial(_attn_b_kernel, blocks_per_step=tq // bq, grid_rows=grid_rows)
    return pl.pallas_call(
        kernel,
        grid=(batch, B_HEADS, nq),
        in_specs=[
            pl.BlockSpec((tq, HEAD_DIM), lambda b, h, i: (b * nq + i, OFF_QB // HEAD_DIM + h)),
            pl.BlockSpec((seq_len, HEAD_DIM), lambda b, h, i: (b, OFF_KB // HEAD_DIM + h)),
            pl.BlockSpec((seq_len, HEAD_DIM), lambda b, h, i: (b, OFF_VB // HEAD_DIM + h)),
            pl.BlockSpec((tq, HEAD_DIM),
                         lambda b, h, i: (b * nq + i, (OFF_GATE + A_Q) // HEAD_DIM + h)),
            pl.BlockSpec((None, None, B_CASES, B_WROWS * GRID_W, bq),
                         lambda b, h, i: (layer, h, 0, 0, 0)),
        ],
        out_specs=pl.BlockSpec((tq, HEAD_DIM), lambda b, h, i: (b * nq + i, h)),
        out_shape=jax.ShapeDtypeStruct((batch * seq_len, B_W), BF16),
        scratch_shapes=[pltpu.VMEM((n_blocks, HEAD_DIM + ONES_ROWS, bq), BF16)],
        compiler_params=pltpu.CompilerParams(
            dimension_semantics=("parallel", "parallel", "arbitrary"),
            vmem_limit_bytes=VMEM_LIMIT_BYTES),
        name="attn_b",
    )(proj, proj, proj, proj, bias_blocks)


def _bias_blocks(rel_bias):
    layers, heads = rel_bias.shape[:2]
    n_dr, n_dc = 2 * NA_ROWS - 1, 2 * NA_COLS - 1
    c = np.arange(GRID_W)
    col_start = np.clip(c - NA_COLS // 2, 0, GRID_W - NA_COLS)
    col_ok = (c[None, :] >= col_start[:, None]) & (c[None, :] < col_start[:, None] + NA_COLS)
    dc = np.clip(c[None, :] - c[:, None], -(NA_COLS - 1), NA_COLS - 1) + (NA_COLS - 1)
    any_rows = 8 * B_QROWS
    pick_dr, row_ok = [], []
    for first_row, win_row in ((0, 0), (2 * B_QROWS, B_QROWS), (any_rows - B_QROWS,
                                                                 any_rows - B_WROWS)):
        r = first_row + np.arange(B_QROWS)
        kr = win_row + np.arange(B_WROWS)
        band0 = np.clip(r - NA_ROWS // 2, 0, any_rows - NA_ROWS)
        row_ok.append((kr[None, :] >= band0[:, None]) & (kr[None, :] < band0[:, None] + NA_ROWS))
        dr = np.clip(kr[None, :] - r[:, None] + (NA_ROWS - 1), 0, n_dr - 1)
        pick_dr.append(dr.reshape(-1, 1) == np.arange(n_dr)[None, :])
    pick_dr = np.stack(pick_dr).astype(np.float32)
    pick_dc = (dc.reshape(-1, 1) == np.arange(n_dc)[None, :]).astype(np.float32)
    bias = jnp.einsum("lhij,cpi,qj->lhcpq", rel_bias.astype(F32), pick_dr, pick_dc,
                      precision=lax.Precision.HIGHEST)
    bias = bias.reshape(layers, heads, B_CASES, B_QROWS, B_WROWS, GRID_W, GRID_W)
    ok = np.stack(row_ok)[:, :, :, None, None] & col_ok[None, None, None, :, :]
    bias = jnp.where(ok[None, None], bias * math.log2(math.e), NEG_INF)
    return jnp.transpose(bias, (0, 1, 2, 4, 6, 3, 5)).reshape(
        layers, heads, B_CASES, B_WROWS * GRID_W, B_QROWS * GRID_W)


def _attn_c_kernel(q_ref, qn_ref, k_ref, v_ref, gate_ref, lq1_ref, lk1_ref, lq2_ref, lk2_ref,
                   sg_ref, o_ref, qt_scr, vt_scr, s_scr, m1_scr, a1_scr, m2_scr, a2_scr,
                   *, tk, n_kv, lam_init):
    dv = 2 * HEAD_DIM
    streams = ((m1_scr, a1_scr), (m2_scr, a2_scr))
    i = pl.program_id(2)
    cur = i % 2
    assert n_kv % 2 == 0

    def load_qt(ref, slot, u):
        qt_scr[slot, u] = ref[:, u * HEAD_DIM:(u + 1) * HEAD_DIM].astype(F32).T.astype(BF16)

    def scores(j, slot, u):
        start = pl.multiple_of(j * tk, tk)
        k = k_ref[pl.ds(start, tk), u * HEAD_DIM:(u + 1) * HEAD_DIM]
        return jnp.dot(k, qt_scr[slot, u], preferred_element_type=F32)

    @pl.when(i == 0)
    def _():
        _build_vt(v_ref, vt_scr, n_kv=n_kv, tk=tk, dv=dv)
        for u in range(2):
            load_qt(q_ref, 0, u)
            s_scr[0, u] = scores(0, 0, u)

    for m_scr, a_scr in streams:
        m_scr[...] = jnp.full(m_scr.shape, NEG_INF, F32)
        a_scr[...] = jnp.zeros(a_scr.shape, F32)

    unroll = _kv_unroll(n_kv, KV_UNROLL // 2)

    def step_group(jj, is_tail):
        for t in range(unroll):
            j = unroll * jj + t
            slot = t % 2
            for u, (m_scr, a_scr) in enumerate(streams):
                if is_tail and t == unroll - 1:
                    load_qt(qn_ref, 1 - cur, u)
                    s_scr[0, u] = scores(0, 1 - cur, u)
                else:
                    s_scr[1 - slot, u] = scores(j + 1, cur, u)
                _flash_step(s_scr[slot, u], vt_scr[j], m_scr, a_scr)

    def body(jj, carry):
        step_group(jj, False)
        return carry

    lax.fori_loop(0, n_kv // unroll - 1, body, 0)
    step_group(n_kv // unroll - 1, True)

    lam = (jnp.exp(jnp.sum(lq1_ref[...] * lk1_ref[...], axis=-1, keepdims=True))
           - jnp.exp(jnp.sum(lq2_ref[...] * lk2_ref[...], axis=-1, keepdims=True))
           + lam_init)
    a1 = a1_scr[...]
    a2 = a2_scr[...]
    o = (a1[0:dv] / a1[dv:dv + 1] - lam * (a2[0:dv] / a2[dv:dv + 1])).T
    ms = jnp.sum(o * o, axis=-1, keepdims=True) * (1.0 / (2 * HEAD_DIM))
    o = (o * lax.rsqrt(ms + EPS)) * sg_ref[...] * (1.0 - lam_init)
    o_ref[...] = (o * _silu(gate_ref[...].astype(F32))).astype(BF16)


def _attn_c(proj, lq1, lk1, lq2, lk2, subln_g, lam_init, batch, seq_len):
    tq, tk = C_TQ, C_TK
    assert seq_len % tq == 0 and seq_len % tk == 0
    nq = seq_len // tq
    n_kv = seq_len // tk
    hw = 2 * HEAD_DIM
    kernel = functools.partial(_attn_c_kernel, tk=tk, n_kv=n_kv, lam_init=lam_init)
    vec = pl.BlockSpec((1, HEAD_DIM), lambda b, h, i: (0, 0))
    return pl.pallas_call(
        kernel,
        grid=(batch, C_HEADS, nq),
        in_specs=[
            pl.BlockSpec((tq, hw), lambda b, h, i: (b * nq + i, OFF_QC // hw + h)),
            pl.BlockSpec((tq, hw),
                         lambda b, h, i: (b * nq + jnp.minimum(i + 1, nq - 1), OFF_QC // hw + h)),
            pl.BlockSpec((seq_len, hw), lambda b, h, i: (b, OFF_KC // hw + h)),
            pl.BlockSpec((seq_len, hw), lambda b, h, i: (b, OFF_VC // hw + h)),
            pl.BlockSpec((tq, hw), lambda b, h, i: (b * nq + i, (OFF_GATE + A_Q + B_W) // hw + h)),
            vec, vec, vec, vec,
            pl.BlockSpec((1, hw), lambda b, h, i: (0, 0)),
        ],
        out_specs=pl.BlockSpec((tq, hw), lambda b, h, i: (b * nq + i, h)),
        out_shape=jax.ShapeDtypeStruct((batch * seq_len, C_V), BF16),
        scratch_shapes=[
            pltpu.VMEM((2, 2, HEAD_DIM, tq), BF16),
            pltpu.VMEM((n_kv, hw + ONES_ROWS, tk), BF16),
            pltpu.VMEM((2, 2, tk, tq), F32),
            pltpu.VMEM((1, tq), F32), pltpu.VMEM((hw + ONES_ROWS, tq), F32),
            pltpu.VMEM((1, tq), F32), pltpu.VMEM((hw + ONES_ROWS, tq), F32),
        ],
        compiler_params=pltpu.CompilerParams(
            dimension_semantics=("parallel", "parallel", "arbitrary"),
            vmem_limit_bytes=VMEM_LIMIT_BYTES),
        name="attn_c",
    )(proj, proj, proj, proj, proj, lq1.reshape(1, HEAD_DIM), lk1.reshape(1, HEAD_DIM),
      lq2.reshape(1, HEAD_DIM), lk2.reshape(1, HEAD_DIM), subln_g.reshape(1, hw))


def _out_proj_kernel(a_ref, b_ref, c_ref, w_ref, x_ref, g_ref, o_ref,
                     y_scr, ss_scr, *, n_row_tiles, n_col_tiles):
    i = pl.program_id(0)
    j = pl.program_id(1)
    cur = i % 2
    prev = 1 - cur

    def matmul():
        y = jnp.dot(a_ref[...], w_ref[0:A_Q, :], preferred_element_type=F32)
        y = y + jnp.dot(b_ref[...], w_ref[A_Q:A_Q + B_W, :], preferred_element_type=F32)
        y = y + jnp.dot(c_ref[...], w_ref[A_Q + B_W:D_MIX, :], preferred_element_type=F32)
        y_scr[cur, j] = y
        ss_scr[cur, j] = jnp.sum(y * y, axis=-1, keepdims=True)

    def finish():
        ss = ss_scr[prev, 0]
        for jj in range(1, n_col_tiles):
            ss = ss + ss_scr[prev, jj]
        inv = lax.rsqrt(ss * (1.0 / D_MODEL) + EPS)
        o_ref[...] = x_ref[...] + (y_scr[prev, j] * inv) * g_ref[...]

    pl.when(i == 0)(matmul)

    @pl.when(jnp.logical_and(i > 0, i < n_row_tiles))
    def _():
        finish()
        matmul()

    pl.when(i == n_row_tiles)(finish)


def _out_proj(mix_a, mix_b, mix_c, w_out_bf, layer, x2d, post_g):
    tokens = x2d.shape[0]
    tm, tn = OUT_TM, OUT_TN
    assert tokens % tm == 0 and D_MODEL % tn == 0
    n_row_tiles = tokens // tm
    n_col_tiles = D_MODEL // tn
    kernel = functools.partial(_out_proj_kernel, n_row_tiles=n_row_tiles, n_col_tiles=n_col_tiles)

    def lhs_row(i, j):
        return (jnp.minimum(i, n_row_tiles - 1), 0)

    def done_tile(i, j):
        return (jnp.maximum(i - 1, 0), jnp.where(i == 0, 0, j))

    return pl.pallas_call(
        kernel,
        grid=(n_row_tiles + 1, n_col_tiles),
        in_specs=[
            pl.BlockSpec((tm, A_Q), lhs_row),
            pl.BlockSpec((tm, B_W), lhs_row),
            pl.BlockSpec((tm, C_V), lhs_row),
            pl.BlockSpec((None, D_MIX, tn), lambda i, j: (layer, 0, j)),
            pl.BlockSpec((tm, tn), done_tile),
            pl.BlockSpec((1, tn), lambda i, j: (0, j)),
        ],
        out_specs=pl.BlockSpec((tm, tn), done_tile),
        out_shape=jax.ShapeDtypeStruct((tokens, D_MODEL), F32),
        scratch_shapes=[pltpu.VMEM((2, n_col_tiles, tm, tn), F32),
                        pltpu.VMEM((2, n_col_tiles, tm, 1), F32)],
        compiler_params=pltpu.CompilerParams(
            dimension_semantics=("arbitrary", "arbitrary"),
            vmem_limit_bytes=VMEM_LIMIT_BYTES),
        name="out_proj",
    )(mix_a, mix_b, mix_c, w_out_bf, x2d, post_g.reshape(1, D_MODEL))


def _angles(pos, dim, theta):
    inv = jnp.power(theta, -jnp.arange(0, dim, 2, dtype=F32) / dim)
    return pos.astype(F32)[:, None] * inv[None, :]


def _rope_tables(seq_len):
    t = jnp.arange(seq_len)
    ang_row = _angles(t // GRID_W, HEAD_DIM // 2, AXIAL_THETA)
    ang_col = _angles(t % GRID_W, HEAD_DIM // 2, AXIAL_THETA)
    cr, sr, cc_, sc_ = jnp.cos(ang_row), jnp.sin(ang_row), jnp.cos(ang_col), jnp.sin(ang_col)
    cos_ax = jnp.concatenate([cr, cr, cc_, cc_], axis=-1)
    sin_ax = jnp.concatenate([-sr, sr, -sc_, sc_], axis=-1)
    ang_t = _angles(t, ROPE_DIMS, ROPE_THETA)
    ct, st = jnp.cos(ang_t), jnp.sin(ang_t)
    rest = HEAD_DIM - ROPE_DIMS
    cos_p = jnp.concatenate([ct, ct, jnp.ones((seq_len, rest), F32)], axis=-1)
    sin_p = jnp.concatenate([-st, st, jnp.zeros((seq_len, rest), F32)], axis=-1)
    return cos_ax, sin_ax, cos_p, sin_p


def _trunk(x, params, w_in_bf, w_out_bf, bias_blocks):
    batch, seq_len, _ = x.shape
    tables = _rope_tables(seq_len)
    x2d = x.reshape(batch * seq_len, D_MODEL)
    depth = w_in_bf.shape[0]
    for l in range(depth):
        lam_init = 0.8 - 0.6 * math.exp(-0.3 * l)
        proj = _in_proj(x2d, params["pre_norm_g"][l], w_in_bf, l, params["a_q_norm_g"][l],
                        params["a_k_norm_g"][l], tables, seq_len)
        mix_a = _attn_a(proj, batch, seq_len)
        mix_b = _attn_b(proj, bias_blocks, l, batch, seq_len)
        mix_c = _attn_c(proj, params["c_lambda_q1"][l], params["c_lambda_k1"][l],
                        params["c_lambda_q2"][l], params["c_lambda_k2"][l],
                        params["c_subln_g"][l], lam_init, batch, seq_len)
        x2d = _out_proj(mix_a, mix_b, mix_c, w_out_bf, l, x2d, params["post_norm_g"][l])
    return x2d.reshape(batch, seq_len, D_MODEL)


def _prepare(params):
    w_in_bf = params["w_in"].astype(BF16)
    w_out_bf = params["w_out"].astype(BF16)
    return w_in_bf, w_out_bf, _bias_blocks(params["b_rel_bias"])


def kernel(x_prompt, x_sample, pre_norm_g, post_norm_g, w_in, w_out, a_q_norm_g, a_k_norm_g,
           b_rel_bias, c_lambda_q1, c_lambda_k1, c_lambda_q2, c_lambda_k2, c_subln_g):
    params = dict(pre_norm_g=pre_norm_g, post_norm_g=post_norm_g, a_q_norm_g=a_q_norm_g,
                  a_k_norm_g=a_k_norm_g, c_lambda_q1=c_lambda_q1, c_lambda_k1=c_lambda_k1,
                  c_lambda_q2=c_lambda_q2, c_lambda_k2=c_lambda_k2, c_subln_g=c_subln_g)
    params.update(w_in=w_in, w_out=w_out, b_rel_bias=b_rel_bias)
    prepared = _prepare(params)
    y_prompt = _trunk(x_prompt, params, *prepared)
    y_sample = _trunk(x_sample, params, *prepared)
    return (y_prompt, y_sample)
```

```python
import functools
import math

import jax
import jax.numpy as jnp
import numpy as np
from jax import lax
from jax.experimental import pallas as pl
from jax.experimental.pallas import tpu as pltpu

F32 = jnp.float32
BF16 = jnp.bfloat16

D_MODEL = 4096
HEAD_DIM = 128
GRID_W = 64
A_HEADS = 12
A_KV_HEADS = 4
A_GROUP = A_HEADS // A_KV_HEADS
B_HEADS = 8
C_HEADS = 6
A_Q = A_HEADS * HEAD_DIM
A_KV = A_KV_HEADS * HEAD_DIM
B_W = B_HEADS * HEAD_DIM
C_QK = C_HEADS * 2 * HEAD_DIM
C_V = C_HEADS * 2 * HEAD_DIM
D_MIX = A_Q + B_W + C_V
D_IN = A_Q + 2 * A_KV + 3 * B_W + 2 * C_QK + C_V + D_MIX
NA_ROWS = 8
NA_COLS = 16
AXIAL_THETA = 10000.0
ROPE_THETA = 500000.0
ROPE_DIMS = HEAD_DIM // 4
EPS = 1e-6
NEG_INF = -1e30
SCALE = 1.0 / math.sqrt(HEAD_DIM)
SCALE_LOG2 = SCALE * math.log2(math.e)
ONES_ROWS = 16

OFF_QA = 0
OFF_KA = OFF_QA + A_Q
OFF_VA = OFF_KA + A_KV
OFF_QB = OFF_VA + A_KV
OFF_KB = OFF_QB + B_W
OFF_VB = OFF_KB + B_W
OFF_QC = OFF_VB + B_W
OFF_KC = OFF_QC + C_QK
OFF_VC = OFF_KC + C_QK
OFF_GATE = OFF_VC + C_V

VMEM_LIMIT_BYTES = 56 * 1024 * 1024

IN_TM = 512
IN_TN = 512
IN_SEG = 512
OUT_TM = 512
OUT_TN = 1024
A_TQ = 256
A_STREAMS = 2
A_TK = 512
B_TQ = 2048
B_QROWS = 4
B_WROWS = B_QROWS + NA_ROWS
B_CASES = 3
C_TQ = 512
C_TK = 512
KV_UNROLL = 16


def _swap_halves(x, h):
    lane = lax.broadcasted_iota(jnp.int32, x.shape, 1)
    first = (lane % (2 * h)) < h
    return jnp.where(first, pltpu.roll(x, HEAD_DIM - h, 1), pltpu.roll(x, h, 1))


def _silu(g):
    return g / (1.0 + jnp.exp(-g))


def _segment_kind(off):
    if off < OFF_KA:
        return "qa"
    if off < OFF_VA:
        return "ka"
    if OFF_QB <= off < OFF_KB:
        return "qb"
    if OFF_QC <= off < OFF_KC:
        return "qc"
    if OFF_KC <= off < OFF_VC:
        return "kc"
    return "plain"


def _in_tile_kinds(n_tiles):
    segs = IN_TN // IN_SEG
    return [tuple(_segment_kind((t * segs + s) * IN_SEG) for s in range(segs))
            for t in range(n_tiles)]


def _in_proj_kernel(x_ref, g_ref, w_ref, qn_ref, kn_ref, ca_ref, sa_ref, cc_ref, sc_ref,
                    o_ref, h_scr, acc_scr, *, n_tiles):
    j = pl.program_id(1)
    cur = j % 2
    prev = 1 - cur
    heads = IN_SEG // HEAD_DIM

    def pre_norm():
        x = x_ref[...]
        ms = jnp.sum(x * x, axis=-1, keepdims=True) * (1.0 / D_MODEL)
        h_scr[...] = ((x * lax.rsqrt(ms + EPS)) * g_ref[...]).astype(BF16)

    def matmul():
        acc_scr[cur] = jnp.dot(h_scr[...], w_ref[...], preferred_element_type=F32)

    def first_step():
        pre_norm()
        matmul()

    def head_norm(y, g):
        ms = jnp.sum(y * y, axis=-1, keepdims=True) * (1.0 / HEAD_DIM)
        return (y * lax.rsqrt(ms + EPS)) * g

    def axial(y):
        return y * ca_ref[...] + _swap_halves(y, HEAD_DIM // 4) * sa_ref[...]

    def partial(y):
        return y * cc_ref[...] + _swap_halves(y, ROPE_DIMS // 2) * sc_ref[...]

    epilogues = {
        "qa": lambda y: axial(head_norm(y, qn_ref[...])) * SCALE_LOG2,
        "ka": lambda y: axial(head_norm(y, kn_ref[...])),
        "qb": lambda y: y * SCALE_LOG2,
        "qc": lambda y: partial(y) * SCALE_LOG2,
        "kc": partial,
        "plain": lambda y: y,
    }

    def finish(kinds):
        for seg, kind in enumerate(kinds):
            for hh in range(heads):
                lo = seg * IN_SEG + hh * HEAD_DIM
                o_ref[seg * heads + hh] = epilogues[kind](
                    acc_scr[prev, :, lo:lo + HEAD_DIM]).astype(BF16)

    def step(kinds):
        finish(kinds)
        matmul()

    tile_kinds = _in_tile_kinds(n_tiles)
    pl.when(j == 0)(first_step)
    for kinds in sorted(set(tile_kinds[:-1])):
        tiles = [t for t in range(n_tiles - 1) if tile_kinds[t] == kinds]
        cond = functools.reduce(jnp.logical_or, [j == t + 1 for t in tiles])
        pl.when(cond)(functools.partial(step, kinds))
    pl.when(j == n_tiles)(functools.partial(finish, tile_kinds[-1]))


def _in_proj(x2d, pre_g, w_in_bf, layer, qn_g, kn_g, tables, seq_len):
    tokens = x2d.shape[0]
    tm = IN_TM
    assert tokens % tm == 0 and seq_len % tm == 0 and D_IN % IN_TN == 0 and IN_TN % IN_SEG == 0
    for off in (OFF_KA, OFF_VA, OFF_QB, OFF_KB, OFF_QC, OFF_KC, OFF_VC):
        assert off % IN_SEG == 0
    n_tiles = D_IN // IN_TN
    pos_blocks = seq_len // tm
    tab_spec = pl.BlockSpec((tm, HEAD_DIM), lambda i, j: (i % pos_blocks, 0))
    vec_spec = pl.BlockSpec((1, HEAD_DIM), lambda i, j: (0, 0))
    return pl.pallas_call(
        functools.partial(_in_proj_kernel, n_tiles=n_tiles),
        grid=(tokens // tm, n_tiles + 1),
        in_specs=[
            pl.BlockSpec((tm, D_MODEL), lambda i, j: (i, 0)),
            pl.BlockSpec((1, D_MODEL), lambda i, j: (0, 0)),
            pl.BlockSpec((None, None, D_MODEL, IN_TN),
                         lambda i, j: (layer, jnp.minimum(j, n_tiles - 1), 0, 0)),
            vec_spec, vec_spec, tab_spec, tab_spec, tab_spec, tab_spec,
        ],
        out_specs=pl.BlockSpec((IN_TN // HEAD_DIM, tm, HEAD_DIM),
                               lambda i, j: (jnp.maximum(j - 1, 0), i, 0)),
        out_shape=jax.ShapeDtypeStruct((D_IN // HEAD_DIM, tokens, HEAD_DIM), BF16),
        scratch_shapes=[pltpu.VMEM((tm, D_MODEL), BF16), pltpu.VMEM((2, tm, IN_TN), F32)],
        compiler_params=pltpu.CompilerParams(
            dimension_semantics=("parallel", "arbitrary"),
            vmem_limit_bytes=VMEM_LIMIT_BYTES),
        name="in_proj",
    )(x2d, pre_g.reshape(1, D_MODEL), w_in_bf, qn_g.reshape(1, HEAD_DIM),
      kn_g.reshape(1, HEAD_DIM), *tables)


def _build_vt(v_slabs, vt_scr, *, n_kv, tk):
    dv = len(v_slabs) * HEAD_DIM

    def chunk(c, carry):
        off = pl.multiple_of(c * tk, tk)
        for n, slab in enumerate(v_slabs):
            vt_scr[c, n * HEAD_DIM:(n + 1) * HEAD_DIM, :] = (
                slab[pl.ds(off, tk), :].astype(F32).T.astype(BF16))
        row = lax.broadcasted_iota(jnp.int32, (ONES_ROWS, tk), 0)
        vt_scr[c, dv:dv + ONES_ROWS, :] = jnp.where(row == 0, 1.0, 0.0).astype(BF16)
        return carry

    lax.fori_loop(0, n_kv, chunk, 0)


def _kv_unroll(n_kv, cap):
    unroll = min(cap, n_kv)
    assert unroll % 2 == 0 and n_kv % unroll == 0
    return unroll


def _flash_step(s, vt, m_scr, acc_scr):
    m_old = m_scr[...]
    m_new = jnp.maximum(m_old, jnp.max(s, axis=0, keepdims=True))
    alpha = jnp.exp2(m_old - m_new)
    p = jnp.exp2(s - m_new).astype(BF16)
    acc_scr[...] = alpha * acc_scr[...] + jnp.dot(vt, p, preferred_element_type=F32)
    m_scr[...] = m_new


def _attn_a_kernel(q_ref, qn_ref, k_ref, v_ref, g0_ref, g1_ref, g2_ref, o_ref,
                   qt_scr, vt_scr, s_scr, m_scr, acc_scr, *, tq, tk, n_kv):
    gate_refs = (g0_ref, g1_ref, g2_ref)
    i = pl.program_id(2)
    cur = i % 2
    assert n_kv % 2 == 0
    streams = range(A_STREAMS)

    def load_qt(ref, slot, st):
        for g in range(A_GROUP):
            qt_scr[slot, st, :, g * tq:(g + 1) * tq] = (
                ref[g, st * tq:(st + 1) * tq, :].astype(F32).T.astype(BF16))

    def scores(j, slot, st):
        start = pl.multiple_of(j * tk, tk)
        return jnp.dot(k_ref[pl.ds(start, tk), :], qt_scr[slot, st],
                       preferred_element_type=F32)

    @pl.when(i == 0)
    def _():
        _build_vt([v_ref], vt_scr, n_kv=n_kv, tk=tk)
        for st in streams:
            load_qt(q_ref, 0, st)
            s_scr[0, st] = scores(0, 0, st)

    m_scr[...] = jnp.full(m_scr.shape, NEG_INF, F32)
    acc_scr[...] = jnp.zeros(acc_scr.shape, F32)

    unroll = _kv_unroll(n_kv, KV_UNROLL // A_STREAMS)

    def step_group(jj, is_tail):
        for u in range(unroll):
            j = unroll * jj + u
            slot = u % 2
            for st in streams:
                if is_tail and u == unroll - 1:
                    load_qt(qn_ref, 1 - cur, st)
                    s_scr[0, st] = scores(0, 1 - cur, st)
                else:
                    s_scr[1 - slot, st] = scores(j + 1, cur, st)
                _flash_step(s_scr[slot, st], vt_scr[j], m_scr.at[st], acc_scr.at[st])

    def body(jj, carry):
        step_group(jj, False)
        return carry

    lax.fori_loop(0, n_kv // unroll - 1, body, 0)
    step_group(n_kv // unroll - 1, True)
    for st in streams:
        acc = acc_scr[st]
        out = (acc[0:HEAD_DIM] / acc[HEAD_DIM:HEAD_DIM + 1]).T
        rows = slice(st * tq, (st + 1) * tq)
        for g in range(A_GROUP):
            sl = slice(g * HEAD_DIM, (g + 1) * HEAD_DIM)
            gate = gate_refs[g][rows, :].astype(F32)
            o_ref[rows, sl] = (out[g * tq:(g + 1) * tq, :] * _silu(gate)).astype(BF16)


def _attn_a(proj, batch, seq_len):
    tq, tk = A_TQ, A_TK
    tile = A_STREAMS * tq
    assert seq_len % tile == 0 and seq_len % tk == 0
    nq = seq_len // tile
    n_kv = seq_len // tk
    gw = A_GROUP * HEAD_DIM
    kernel = functools.partial(_attn_a_kernel, tq=tq, tk=tk, n_kv=n_kv)

    def gate_spec(g):
        return pl.BlockSpec(
            (None, tile, HEAD_DIM),
            lambda b, h, i: (OFF_GATE // HEAD_DIM + A_GROUP * h + g, b * nq + i, 0))

    return pl.pallas_call(
        kernel,
        grid=(batch, A_KV_HEADS, nq),
        in_specs=[
            pl.BlockSpec((A_GROUP, tile, HEAD_DIM), lambda b, h, i: (h, b * nq + i, 0)),
            pl.BlockSpec((A_GROUP, tile, HEAD_DIM),
                         lambda b, h, i: (h, b * nq + jnp.minimum(i + 1, nq - 1), 0)),
            pl.BlockSpec((None, seq_len, HEAD_DIM),
                         lambda b, h, i: (OFF_KA // HEAD_DIM + h, b, 0)),
            pl.BlockSpec((None, seq_len, HEAD_DIM),
                         lambda b, h, i: (OFF_VA // HEAD_DIM + h, b, 0)),
            gate_spec(0), gate_spec(1), gate_spec(2),
        ],
        out_specs=pl.BlockSpec((tile, gw), lambda b, h, i: (b * nq + i, h)),
        out_shape=jax.ShapeDtypeStruct((batch * seq_len, A_Q), BF16),
        scratch_shapes=[
            pltpu.VMEM((2, A_STREAMS, HEAD_DIM, A_GROUP * tq), BF16),
            pltpu.VMEM((n_kv, HEAD_DIM + ONES_ROWS, tk), BF16),
            pltpu.VMEM((2, A_STREAMS, tk, A_GROUP * tq), F32),
            pltpu.VMEM((A_STREAMS, 1, A_GROUP * tq), F32),
            pltpu.VMEM((A_STREAMS, HEAD_DIM + ONES_ROWS, A_GROUP * tq), F32),
        ],
        compiler_params=pltpu.CompilerParams(
            dimension_semantics=("parallel", "parallel", "arbitrary"),
            vmem_limit_bytes=VMEM_LIMIT_BYTES),
        name="attn_a",
    )(proj, proj, proj, proj, proj, proj, proj)


def _attn_b_kernel(q_ref, k_ref, v_ref, gate_ref, bias_ref, o_ref, vt_scr,
                   *, blocks_per_step, grid_rows):
    t = pl.program_id(2)
    bq = B_QROWS * GRID_W
    wk = B_WROWS * GRID_W
    n_blocks = grid_rows // B_QROWS

    @pl.when(t == 0)
    def _():
        _build_vt([v_ref], vt_scr, n_kv=n_blocks, tk=bq)

    def block_rows(u):
        return slice(u * bq, (u + 1) * bq)

    def window(u):
        blk = t * blocks_per_step + u
        win_row = jnp.clip(blk * B_QROWS - NA_ROWS // 2, 0, grid_rows - B_WROWS)
        case = jnp.where(blk == 0, 0, jnp.where(blk == n_blocks - 1, 2, 1))
        return win_row, case

    def scores(u):
        win_row, case = window(u)
        qt = q_ref[block_rows(u), :].astype(F32).T.astype(BF16)
        k = k_ref[pl.ds(pl.multiple_of(win_row * GRID_W, bq), wk), :]
        return jnp.dot(k, qt, preferred_element_type=F32) + bias_ref[case]

    def softmax(s):
        return jnp.exp2(s - jnp.max(s, axis=0, keepdims=True)).astype(BF16)

    def weighted_values(u, p):
        chunk0 = window(u)[0] // B_QROWS
        vt = jnp.concatenate([vt_scr[chunk0 + w] for w in range(B_WROWS // B_QROWS)], axis=1)
        return jnp.dot(vt, p, preferred_element_type=F32)

    def store(u, acc):
        o = (acc[0:HEAD_DIM] / acc[HEAD_DIM:HEAD_DIM + 1]).T
        gate = gate_ref[block_rows(u), :].astype(F32)
        o_ref[block_rows(u), :] = (o * _silu(gate)).astype(BF16)

    n = blocks_per_step
    s, p, acc = {0: scores(0)}, {}, {}
    for u in range(n + 2):
        if u + 1 < n:
            s[u + 1] = scores(u + 1)
        if 0 <= u - 1 < n:
            acc[u - 1] = weighted_values(u - 1, p.pop(u - 1))
        if u < n:
            p[u] = softmax(s.pop(u))
        if 0 <= u - 2 < n:
            store(u - 2, acc.pop(u - 2))


def _attn_b(proj, bias_blocks, layer, batch, seq_len):
    tq = min(B_TQ, seq_len)
    bq = B_QROWS * GRID_W
    assert seq_len % tq == 0 and tq % bq == 0
    grid_rows = seq_len // GRID_W
    assert grid_rows % B_QROWS == 0 and grid_rows >= 4 * B_QROWS
    nq = seq_len // tq
    n_blocks = grid_rows // B_QROWS
    kernel = functools.partial(_attn_b_kernel, blocks_per_step=tq // bq, grid_rows=grid_rows)
    return pl.pallas_call(
        kernel,
        grid=(batch, B_HEADS, nq),
        in_specs=[
            pl.BlockSpec((None, tq, HEAD_DIM),
                         lambda b, h, i: (OFF_QB // HEAD_DIM + h, b * nq + i, 0)),
            pl.BlockSpec((None, seq_len, HEAD_DIM),
                         lambda b, h, i: (OFF_KB // HEAD_DIM + h, b, 0)),
            pl.BlockSpec((None, seq_len, HEAD_DIM),
                         lambda b, h, i: (OFF_VB // HEAD_DIM + h, b, 0)),
            pl.BlockSpec((None, tq, HEAD_DIM),
                         lambda b, h, i: ((OFF_GATE + A_Q) // HEAD_DIM + h, b * nq + i, 0)),
            pl.BlockSpec((None, None, B_CASES, B_WROWS * GRID_W, bq),
                         lambda b, h, i: (layer, h, 0, 0, 0)),
        ],
        out_specs=pl.BlockSpec((tq, HEAD_DIM), lambda b, h, i: (b * nq + i, h)),
        out_shape=jax.ShapeDtypeStruct((batch * seq_len, B_W), BF16),
        scratch_shapes=[pltpu.VMEM((n_blocks, HEAD_DIM + ONES_ROWS, bq), BF16)],
        compiler_params=pltpu.CompilerParams(
            dimension_semantics=("parallel", "parallel", "arbitrary"),
            vmem_limit_bytes=VMEM_LIMIT_BYTES),
        name="attn_b",
    )(proj, proj, proj, proj, bias_blocks)


def _bias_blocks(rel_bias):
    layers, heads = rel_bias.shape[:2]
    n_dr, n_dc = 2 * NA_ROWS - 1, 2 * NA_COLS - 1
    c = np.arange(GRID_W)
    col_start = np.clip(c - NA_COLS // 2, 0, GRID_W - NA_COLS)
    col_ok = (c[None, :] >= col_start[:, None]) & (c[None, :] < col_start[:, None] + NA_COLS)
    dc = np.clip(c[None, :] - c[:, None], -(NA_COLS - 1), NA_COLS - 1) + (NA_COLS - 1)
    any_rows = 8 * B_QROWS
    pick_dr, row_ok = [], []
    for first_row, win_row in ((0, 0), (2 * B_QROWS, B_QROWS), (any_rows - B_QROWS,
                                                                 any_rows - B_WROWS)):
        r = first_row + np.arange(B_QROWS)
        kr = win_row + np.arange(B_WROWS)
        band0 = np.clip(r - NA_ROWS // 2, 0, any_rows - NA_ROWS)
        row_ok.append((kr[None, :] >= band0[:, None]) & (kr[None, :] < band0[:, None] + NA_ROWS))
        dr = np.clip(kr[None, :] - r[:, None] + (NA_ROWS - 1), 0, n_dr - 1)
        pick_dr.append(dr.reshape(-1, 1) == np.arange(n_dr)[None, :])
    pick_dr = np.stack(pick_dr).astype(np.float32)
    pick_dc = (dc.reshape(-1, 1) == np.arange(n_dc)[None, :]).astype(np.float32)
    bias = jnp.einsum("lhij,cpi,qj->lhcpq", rel_bias.astype(F32), pick_dr, pick_dc,
                      precision=lax.Precision.HIGHEST)
    bias = bias.reshape(layers, heads, B_CASES, B_QROWS, B_WROWS, GRID_W, GRID_W)
    ok = np.stack(row_ok)[:, :, :, None, None] & col_ok[None, None, None, :, :]
    bias = jnp.where(ok[None, None], bias * math.log2(math.e), NEG_INF)
    return jnp.transpose(bias, (0, 1, 2, 4, 6, 3, 5)).reshape(
        layers, heads, B_CASES, B_WROWS * GRID_W, B_QROWS * GRID_W)


def _attn_c_kernel(q_ref, qn_ref, k_ref, v_ref, gate_ref, lq1_ref, lk1_ref, lq2_ref, lk2_ref,
                   sg_ref, o_ref, qt_scr, vt_scr, s_scr, m1_scr, a1_scr, m2_scr, a2_scr,
                   *, tk, n_kv, lam_init):
    dv = 2 * HEAD_DIM
    streams = ((m1_scr, a1_scr), (m2_scr, a2_scr))
    i = pl.program_id(2)
    cur = i % 2
    assert n_kv % 2 == 0

    def load_qt(ref, slot, u):
        qt_scr[slot, u] = ref[u].astype(F32).T.astype(BF16)

    def scores(j, slot, u):
        start = pl.multiple_of(j * tk, tk)
        return jnp.dot(k_ref[u, pl.ds(start, tk), :], qt_scr[slot, u],
                       preferred_element_type=F32)

    @pl.when(i == 0)
    def _():
        _build_vt([v_ref.at[0], v_ref.at[1]], vt_scr, n_kv=n_kv, tk=tk)
        for u in range(2):
            load_qt(q_ref, 0, u)
            s_scr[0, u] = scores(0, 0, u)

    for m_scr, a_scr in streams:
        m_scr[...] = jnp.full(m_scr.shape, NEG_INF, F32)
        a_scr[...] = jnp.zeros(a_scr.shape, F32)

    unroll = _kv_unroll(n_kv, KV_UNROLL // 2)

    def step_group(jj, is_tail):
        for t in range(unroll):
            j = unroll * jj + t
            slot = t % 2
            for u, (m_scr, a_scr) in enumerate(streams):
                if is_tail and t == unroll - 1:
                    load_qt(qn_ref, 1 - cur, u)
                    s_scr[0, u] = scores(0, 1 - cur, u)
                else:
                    s_scr[1 - slot, u] = scores(j + 1, cur, u)
                _flash_step(s_scr[slot, u], vt_scr[j], m_scr, a_scr)

    def body(jj, carry):
        step_group(jj, False)
        return carry

    lax.fori_loop(0, n_kv // unroll - 1, body, 0)
    step_group(n_kv // unroll - 1, True)

    lam = (jnp.exp(jnp.sum(lq1_ref[...] * lk1_ref[...], axis=-1, keepdims=True))
           - jnp.exp(jnp.sum(lq2_ref[...] * lk2_ref[...], axis=-1, keepdims=True))
           + lam_init)
    a1 = a1_scr[...]
    a2 = a2_scr[...]
    o = (a1[0:dv] / a1[dv:dv + 1] - lam * (a2[0:dv] / a2[dv:dv + 1])).T
    ms = jnp.sum(o * o, axis=-1, keepdims=True) * (1.0 / (2 * HEAD_DIM))
    o = (o * lax.rsqrt(ms + EPS)) * sg_ref[...] * (1.0 - lam_init)
    gate = jnp.concatenate([gate_ref[0], gate_ref[1]], axis=-1).astype(F32)
    o_ref[...] = (o * _silu(gate)).astype(BF16)


def _attn_c(proj, lq1, lk1, lq2, lk2, subln_g, lam_init, batch, seq_len):
    tq, tk = C_TQ, C_TK
    assert seq_len % tq == 0 and seq_len % tk == 0
    nq = seq_len // tq
    n_kv = seq_len // tk
    hw = 2 * HEAD_DIM
    kernel = functools.partial(_attn_c_kernel, tk=tk, n_kv=n_kv, lam_init=lam_init)
    vec = pl.BlockSpec((1, HEAD_DIM), lambda b, h, i: (0, 0))
    return pl.pallas_call(
        kernel,
        grid=(batch, C_HEADS, nq),
        in_specs=[
            pl.BlockSpec((2, tq, HEAD_DIM), lambda b, h, i: (OFF_QC // hw + h, b * nq + i, 0)),
            pl.BlockSpec((2, tq, HEAD_DIM),
                         lambda b, h, i: (OFF_QC // hw + h, b * nq + jnp.minimum(i + 1, nq - 1), 0)),
            pl.BlockSpec((2, seq_len, HEAD_DIM), lambda b, h, i: (OFF_KC // hw + h, b, 0)),
            pl.BlockSpec((2, seq_len, HEAD_DIM), lambda b, h, i: (OFF_VC // hw + h, b, 0)),
            pl.BlockSpec((2, tq, HEAD_DIM),
                         lambda b, h, i: ((OFF_GATE + A_Q + B_W) // hw + h, b * nq + i, 0)),
            vec, vec, vec, vec,
            pl.BlockSpec((1, hw), lambda b, h, i: (0, 0)),
        ],
        out_specs=pl.BlockSpec((tq, hw), lambda b, h, i: (b * nq + i, h)),
        out_shape=jax.ShapeDtypeStruct((batch * seq_len, C_V), BF16),
        scratch_shapes=[
            pltpu.VMEM((2, 2, HEAD_DIM, tq), BF16),
            pltpu.VMEM((n_kv, hw + ONES_ROWS, tk), BF16),
            pltpu.VMEM((2, 2, tk, tq), F32),
            pltpu.VMEM((1, tq), F32), pltpu.VMEM((hw + ONES_ROWS, tq), F32),
            pltpu.VMEM((1, tq), F32), pltpu.VMEM((hw + ONES_ROWS, tq), F32),
        ],
        compiler_params=pltpu.CompilerParams(
            dimension_semantics=("parallel", "parallel", "arbitrary"),
            vmem_limit_bytes=VMEM_LIMIT_BYTES),
        name="attn_c",
    )(proj, proj, proj, proj, proj, lq1.reshape(1, HEAD_DIM), lk1.reshape(1, HEAD_DIM),
      lq2.reshape(1, HEAD_DIM), lk2.reshape(1, HEAD_DIM), subln_g.reshape(1, hw))


def _out_proj_kernel(a_ref, b_ref, c_ref, w_ref, x_ref, g_ref, o_ref,
                     y_scr, ss_scr, *, n_row_tiles, n_col_tiles):
    i = pl.program_id(0)
    j = pl.program_id(1)
    cur = i % 2
    prev = 1 - cur

    def matmul():
        y = jnp.dot(a_ref[...], w_ref[0:A_Q, :], preferred_element_type=F32)
        y = y + jnp.dot(b_ref[...], w_ref[A_Q:A_Q + B_W, :], preferred_element_type=F32)
        y = y + jnp.dot(c_ref[...], w_ref[A_Q + B_W:D_MIX, :], preferred_element_type=F32)
        y_scr[cur, j] = y
        ss_scr[cur, j] = jnp.sum(y * y, axis=-1, keepdims=True)

    def finish():
        ss = ss_scr[prev, 0]
        for jj in range(1, n_col_tiles):
            ss = ss + ss_scr[prev, jj]
        inv = lax.rsqrt(ss * (1.0 / D_MODEL) + EPS)
        o_ref[...] = x_ref[...] + (y_scr[prev, j] * inv) * g_ref[...]

    pl.when(i == 0)(matmul)

    @pl.when(jnp.logical_and(i > 0, i < n_row_tiles))
    def _():
        finish()
        matmul()

    pl.when(i == n_row_tiles)(finish)


def _out_proj(mix_a, mix_b, mix_c, w_out_bf, layer, x2d, post_g):
    tokens = x2d.shape[0]
    tm, tn = OUT_TM, OUT_TN
    assert tokens % tm == 0 and D_MODEL % tn == 0
    n_row_tiles = tokens // tm
    n_col_tiles = D_MODEL // tn
    kernel = functools.partial(_out_proj_kernel, n_row_tiles=n_row_tiles, n_col_tiles=n_col_tiles)

    def lhs_row(i, j):
        return (jnp.minimum(i, n_row_tiles - 1), 0)

    def done_tile(i, j):
        return (jnp.maximum(i - 1, 0), jnp.where(i == 0, 0, j))

    return pl.pallas_call(
        kernel,
        grid=(n_row_tiles + 1, n_col_tiles),
        in_specs=[
            pl.BlockSpec((tm, A_Q), lhs_row),
            pl.BlockSpec((tm, B_W), lhs_row),
            pl.BlockSpec((tm, C_V), lhs_row),
            pl.BlockSpec((None, None, D_MIX, tn), lambda i, j: (layer, j, 0, 0)),
            pl.BlockSpec((tm, tn), done_tile),
            pl.BlockSpec((1, tn), lambda i, j: (0, j)),
        ],
        out_specs=pl.BlockSpec((tm, tn), done_tile),
        out_shape=jax.ShapeDtypeStruct((tokens, D_MODEL), F32),
        scratch_shapes=[pltpu.VMEM((2, n_col_tiles, tm, tn), F32),
                        pltpu.VMEM((2, n_col_tiles, tm, 1), F32)],
        compiler_params=pltpu.CompilerParams(
            dimension_semantics=("arbitrary", "arbitrary"),
            vmem_limit_bytes=VMEM_LIMIT_BYTES),
        name="out_proj",
    )(mix_a, mix_b, mix_c, w_out_bf, x2d, post_g.reshape(1, D_MODEL))


def _angles(pos, dim, theta):
    inv = jnp.power(theta, -jnp.arange(0, dim, 2, dtype=F32) / dim)
    return pos.astype(F32)[:, None] * inv[None, :]


def _rope_tables(seq_len):
    t = jnp.arange(seq_len)
    ang_row = _angles(t // GRID_W, HEAD_DIM // 2, AXIAL_THETA)
    ang_col = _angles(t % GRID_W, HEAD_DIM // 2, AXIAL_THETA)
    cr, sr, cc_, sc_ = jnp.cos(ang_row), jnp.sin(ang_row), jnp.cos(ang_col), jnp.sin(ang_col)
    cos_ax = jnp.concatenate([cr, cr, cc_, cc_], axis=-1)
    sin_ax = jnp.concatenate([-sr, sr, -sc_, sc_], axis=-1)
    ang_t = _angles(t, ROPE_DIMS, ROPE_THETA)
    ct, st = jnp.cos(ang_t), jnp.sin(ang_t)
    rest = HEAD_DIM - ROPE_DIMS
    cos_p = jnp.concatenate([ct, ct, jnp.ones((seq_len, rest), F32)], axis=-1)
    sin_p = jnp.concatenate([-st, st, jnp.zeros((seq_len, rest), F32)], axis=-1)
    return cos_ax, sin_ax, cos_p, sin_p


def _trunk(x, params, w_in_bf, w_out_bf, bias_blocks):
    batch, seq_len, _ = x.shape
    tables = _rope_tables(seq_len)
    x2d = x.reshape(batch * seq_len, D_MODEL)
    depth = w_in_bf.shape[0]
    for l in range(depth):
        lam_init = 0.8 - 0.6 * math.exp(-0.3 * l)
        proj = _in_proj(x2d, params["pre_norm_g"][l], w_in_bf, l, params["a_q_norm_g"][l],
                        params["a_k_norm_g"][l], tables, seq_len)
        mix_a = _attn_a(proj, batch, seq_len)
        mix_b = _attn_b(proj, bias_blocks, l, batch, seq_len)
        mix_c = _attn_c(proj, params["c_lambda_q1"][l], params["c_lambda_k1"][l],
                        params["c_lambda_q2"][l], params["c_lambda_k2"][l],
                        params["c_subln_g"][l], lam_init, batch, seq_len)
        x2d = _out_proj(mix_a, mix_b, mix_c, w_out_bf, l, x2d, params["post_norm_g"][l])
    return x2d.reshape(batch, seq_len, D_MODEL)


def _prepare(params):
    def tile_major(w, tn):
        layers, rows, cols = w.shape
        return w.astype(BF16).reshape(layers, rows, cols // tn, tn).transpose(0, 2, 1, 3)

    return (tile_major(params["w_in"], IN_TN), tile_major(params["w_out"], OUT_TN),
            _bias_blocks(params["b_rel_bias"]))


def kernel(x_prompt, x_sample, pre_norm_g, post_norm_g, w_in, w_out, a_q_norm_g, a_k_norm_g,
           b_rel_bias, c_lambda_q1, c_lambda_k1, c_lambda_q2, c_lambda_k2, c_subln_g):
    params = dict(pre_norm_g=pre_norm_g, post_norm_g=post_norm_g, a_q_norm_g=a_q_norm_g,
                  a_k_norm_g=a_k_norm_g, c_lambda_q1=c_lambda_q1, c_lambda_k1=c_lambda_k1,
                  c_lambda_q2=c_lambda_q2, c_lambda_k2=c_lambda_k2, c_subln_g=c_subln_g)
    params.update(w_in=w_in, w_out=w_out, b_rel_bias=b_rel_bias)
    prepared = _prepare(params)
    y_prompt = _trunk(x_prompt, params, *prepared)
    y_sample = _trunk(x_sample, params, *prepared)
    return (y_prompt, y_sample)
```

```python
import functools
import math

import jax
import jax.numpy as jnp
import numpy as np
from jax import lax
from jax.experimental import pallas as pl
from jax.experimental.pallas import tpu as pltpu

F32 = jnp.float32
BF16 = jnp.bfloat16

D_MODEL = 4096
HEAD_DIM = 128
GRID_W = 64
A_HEADS = 12
A_KV_HEADS = 4
A_GROUP = A_HEADS // A_KV_HEADS
B_HEADS = 8
C_HEADS = 6
A_Q = A_HEADS * HEAD_DIM
A_KV = A_KV_HEADS * HEAD_DIM
B_W = B_HEADS * HEAD_DIM
C_QK = C_HEADS * 2 * HEAD_DIM
C_V = C_HEADS * 2 * HEAD_DIM
D_MIX = A_Q + B_W + C_V
D_IN = A_Q + 2 * A_KV + 3 * B_W + 2 * C_QK + C_V + D_MIX
NA_ROWS = 8
NA_COLS = 16
AXIAL_THETA = 10000.0
ROPE_THETA = 500000.0
ROPE_DIMS = HEAD_DIM // 4
EPS = 1e-6
NEG_INF = -1e30
SCALE = 1.0 / math.sqrt(HEAD_DIM)
SCALE_LOG2 = SCALE * math.log2(math.e)
ONES_ROWS = 16

OFF_QA = 0
OFF_KA = OFF_QA + A_Q
OFF_VA = OFF_KA + A_KV
OFF_QB = OFF_VA + A_KV
OFF_KB = OFF_QB + B_W
OFF_VB = OFF_KB + B_W
OFF_QC = OFF_VB + B_W
OFF_KC = OFF_QC + C_QK
OFF_VC = OFF_KC + C_QK
OFF_GATE = OFF_VC + C_V

VMEM_LIMIT_BYTES = 56 * 1024 * 1024

IN_TM = 512
IN_TN = 512
IN_SEG = 512
W_CHUNKS = 4
OUT_TM = 512
OUT_TN = 1024
OUT_WROWS = 512
A_TQ = 256
A_STREAMS = 2
A_TK = 512
B_TQ = 2048
B_QROWS = 4
B_WROWS = B_QROWS + NA_ROWS
B_CASES = 3
C_TQ = 512
C_TK = 512
KV_UNROLL = 16


def _swap_halves(x, h):
    lane = lax.broadcasted_iota(jnp.int32, x.shape, 1)
    first = (lane % (2 * h)) < h
    return jnp.where(first, pltpu.roll(x, HEAD_DIM - h, 1), pltpu.roll(x, h, 1))


def _silu(g):
    return g / (1.0 + jnp.exp(-g))


def _segment_kind(off):
    if off < OFF_KA:
        return "qa"
    if off < OFF_VA:
        return "ka"
    if OFF_QB <= off < OFF_KB:
        return "qb"
    if OFF_QC <= off < OFF_KC:
        return "qc"
    if OFF_KC <= off < OFF_VC:
        return "kc"
    return "plain"


def _in_tile_kinds(n_tiles):
    segs = IN_TN // IN_SEG
    return [tuple(_segment_kind((t * segs + s) * IN_SEG) for s in range(segs))
            for t in range(n_tiles)]


def _in_proj_kernel(x_ref, g_ref, *rest, n_tiles):
    w_refs = rest[:W_CHUNKS]
    qn_ref, kn_ref, ca_ref, sa_ref, cc_ref, sc_ref, o_ref, h_scr, acc_scr = rest[W_CHUNKS:]
    j = pl.program_id(1)
    cur = j % 2
    prev = 1 - cur
    heads = IN_SEG // HEAD_DIM

    def pre_norm():
        x = x_ref[...]
        ms = jnp.sum(x * x, axis=-1, keepdims=True) * (1.0 / D_MODEL)
        h_scr[...] = ((x * lax.rsqrt(ms + EPS)) * g_ref[...]).astype(BF16)

    def matmul():
        rows = D_MODEL // W_CHUNKS
        acc = None
        for r, w_ref in enumerate(w_refs):
            part = jnp.dot(h_scr[:, r * rows:(r + 1) * rows], w_ref[...],
                           preferred_element_type=F32)
            acc = part if acc is None else acc + part
        acc_scr[cur] = acc

    def first_step():
        pre_norm()
        matmul()

    def head_norm(y, g):
        ms = jnp.sum(y * y, axis=-1, keepdims=True) * (1.0 / HEAD_DIM)
        return (y * lax.rsqrt(ms + EPS)) * g

    def axial(y):
        return y * ca_ref[...] + _swap_halves(y, HEAD_DIM // 4) * sa_ref[...]

    def partial(y):
        return y * cc_ref[...] + _swap_halves(y, ROPE_DIMS // 2) * sc_ref[...]

    epilogues = {
        "qa": lambda y: axial(head_norm(y, qn_ref[...])) * SCALE_LOG2,
        "ka": lambda y: axial(head_norm(y, kn_ref[...])),
        "qb": lambda y: y * SCALE_LOG2,
        "qc": lambda y: partial(y) * SCALE_LOG2,
        "kc": partial,
        "plain": lambda y: y,
    }

    def finish(kinds):
        for seg, kind in enumerate(kinds):
            for hh in range(heads):
                lo = seg * IN_SEG + hh * HEAD_DIM
                o_ref[:, lo:lo + HEAD_DIM] = epilogues[kind](
                    acc_scr[prev, :, lo:lo + HEAD_DIM]).astype(BF16)

    def step(kinds):
        finish(kinds)
        matmul()

    tile_kinds = _in_tile_kinds(n_tiles)
    pl.when(j == 0)(first_step)
    for kinds in sorted(set(tile_kinds[:-1])):
        tiles = [t for t in range(n_tiles - 1) if tile_kinds[t] == kinds]
        cond = functools.reduce(jnp.logical_or, [j == t + 1 for t in tiles])
        pl.when(cond)(functools.partial(step, kinds))
    pl.when(j == n_tiles)(functools.partial(finish, tile_kinds[-1]))


def _in_proj(x2d, pre_g, w_in_bf, layer, qn_g, kn_g, tables, seq_len):
    tokens = x2d.shape[0]
    tm = IN_TM
    assert tokens % tm == 0 and seq_len % tm == 0 and D_IN % IN_TN == 0 and IN_TN % IN_SEG == 0
    for off in (OFF_KA, OFF_VA, OFF_QB, OFF_KB, OFF_QC, OFF_KC, OFF_VC):
        assert off % IN_SEG == 0
    n_tiles = D_IN // IN_TN
    pos_blocks = seq_len // tm
    tab_spec = pl.BlockSpec((tm, HEAD_DIM), lambda i, j: (i % pos_blocks, 0))
    vec_spec = pl.BlockSpec((1, HEAD_DIM), lambda i, j: (0, 0))

    def w_spec(r):
        return pl.BlockSpec((None, D_MODEL // W_CHUNKS, IN_TN),
                            lambda i, j: (layer, r, jnp.minimum(j, n_tiles - 1)))

    return pl.pallas_call(
        functools.partial(_in_proj_kernel, n_tiles=n_tiles),
        grid=(tokens // tm, n_tiles + 1),
        in_specs=[
            pl.BlockSpec((tm, D_MODEL), lambda i, j: (i, 0)),
            pl.BlockSpec((1, D_MODEL), lambda i, j: (0, 0)),
            *[w_spec(r) for r in range(W_CHUNKS)],
            vec_spec, vec_spec, tab_spec, tab_spec, tab_spec, tab_spec,
        ],
        out_specs=pl.BlockSpec((tm, IN_TN), lambda i, j: (i, jnp.maximum(j - 1, 0))),
        out_shape=jax.ShapeDtypeStruct((tokens, D_IN), BF16),
        scratch_shapes=[pltpu.VMEM((tm, D_MODEL), BF16), pltpu.VMEM((2, tm, IN_TN), F32)],
        compiler_params=pltpu.CompilerParams(
            dimension_semantics=("parallel", "arbitrary"),
            vmem_limit_bytes=VMEM_LIMIT_BYTES),
        name="in_proj",
    )(x2d, pre_g.reshape(1, D_MODEL), *([w_in_bf] * W_CHUNKS), qn_g.reshape(1, HEAD_DIM),
      kn_g.reshape(1, HEAD_DIM), *tables)


def _build_vt(v_ref, vt_scr, *, n_kv, tk, dv):
    def chunk(c, carry):
        off = pl.multiple_of(c * tk, tk)
        vt_scr[c, 0:dv, :] = v_ref[pl.ds(off, tk), :].astype(F32).T.astype(BF16)
        row = lax.broadcasted_iota(jnp.int32, (ONES_ROWS, tk), 0)
        vt_scr[c, dv:dv + ONES_ROWS, :] = jnp.where(row == 0, 1.0, 0.0).astype(BF16)
        return carry

    lax.fori_loop(0, n_kv, chunk, 0)


def _kv_unroll(n_kv, cap):
    unroll = min(cap, n_kv)
    assert unroll % 2 == 0 and n_kv % unroll == 0
    return unroll


def _flash_step(s, vt, m_scr, acc_scr):
    m_old = m_scr[...]
    m_new = jnp.maximum(m_old, jnp.max(s, axis=0, keepdims=True))
    alpha = jnp.exp2(m_old - m_new)
    p = jnp.exp2(s - m_new).astype(BF16)
    acc_scr[...] = alpha * acc_scr[...] + jnp.dot(vt, p, preferred_element_type=F32)
    m_scr[...] = m_new


def _attn_a_kernel(q_ref, qn_ref, k_ref, v_ref, g0_ref, g1_ref, g2_ref, o_ref,
                   qt_scr, vt_scr, s_scr, m_scr, acc_scr, *, tq, tk, n_kv):
    gate_refs = (g0_ref, g1_ref, g2_ref)
    i = pl.program_id(2)
    cur = i % 2
    assert n_kv % 2 == 0
    streams = range(A_STREAMS)

    def load_qt(ref, slot, st):
        for g in range(A_GROUP):
            qt_scr[slot, st, :, g * tq:(g + 1) * tq] = (
                ref[st * tq:(st + 1) * tq,
                    g * HEAD_DIM:(g + 1) * HEAD_DIM].astype(F32).T.astype(BF16))

    def scores(j, slot, st):
        start = pl.multiple_of(j * tk, tk)
        return jnp.dot(k_ref[pl.ds(start, tk), :], qt_scr[slot, st],
                       preferred_element_type=F32)

    @pl.when(i == 0)
    def _():
        _build_vt(v_ref, vt_scr, n_kv=n_kv, tk=tk, dv=HEAD_DIM)
        for st in streams:
            load_qt(q_ref, 0, st)
            s_scr[0, st] = scores(0, 0, st)

    m_scr[...] = jnp.full(m_scr.shape, NEG_INF, F32)
    acc_scr[...] = jnp.zeros(acc_scr.shape, F32)

    unroll = _kv_unroll(n_kv, KV_UNROLL // A_STREAMS)

    def step_group(jj, is_tail):
        for u in range(unroll):
            j = unroll * jj + u
            slot = u % 2
            for st in streams:
                if is_tail and u == unroll - 1:
                    load_qt(qn_ref, 1 - cur, st)
                    s_scr[0, st] = scores(0, 1 - cur, st)
                else:
                    s_scr[1 - slot, st] = scores(j + 1, cur, st)
                _flash_step(s_scr[slot, st], vt_scr[j], m_scr.at[st], acc_scr.at[st])

    def body(jj, carry):
        step_group(jj, False)
        return carry

    lax.fori_loop(0, n_kv // unroll - 1, body, 0)
    step_group(n_kv // unroll - 1, True)
    for st in streams:
        acc = acc_scr[st]
        out = (acc[0:HEAD_DIM] / acc[HEAD_DIM:HEAD_DIM + 1]).T
        rows = slice(st * tq, (st + 1) * tq)
        for g in range(A_GROUP):
            sl = slice(g * HEAD_DIM, (g + 1) * HEAD_DIM)
            gate = gate_refs[g][rows, :].astype(F32)
            o_ref[rows, sl] = (out[g * tq:(g + 1) * tq, :] * _silu(gate)).astype(BF16)


def _attn_a(proj, batch, seq_len):
    tq, tk = A_TQ, A_TK
    tile = A_STREAMS * tq
    assert seq_len % tile == 0 and seq_len % tk == 0
    nq = seq_len // tile
    n_kv = seq_len // tk
    gw = A_GROUP * HEAD_DIM
    kernel = functools.partial(_attn_a_kernel, tq=tq, tk=tk, n_kv=n_kv)

    def gate_spec(g):
        return pl.BlockSpec(
            (tile, HEAD_DIM),
            lambda b, h, i: (b * nq + i, OFF_GATE // HEAD_DIM + A_GROUP * h + g))

    return pl.pallas_call(
        kernel,
        grid=(batch, A_KV_HEADS, nq),
        in_specs=[
            pl.BlockSpec((tile, gw), lambda b, h, i: (b * nq + i, h)),
            pl.BlockSpec((tile, gw), lambda b, h, i: (b * nq + jnp.minimum(i + 1, nq - 1), h)),
            pl.BlockSpec((seq_len, HEAD_DIM), lambda b, h, i: (b, OFF_KA // HEAD_DIM + h)),
            pl.BlockSpec((seq_len, HEAD_DIM), lambda b, h, i: (b, OFF_VA // HEAD_DIM + h)),
            gate_spec(0), gate_spec(1), gate_spec(2),
        ],
        out_specs=pl.BlockSpec((tile, gw), lambda b, h, i: (b * nq + i, h)),
        out_shape=jax.ShapeDtypeStruct((batch * seq_len, A_Q), BF16),
        scratch_shapes=[
            pltpu.VMEM((2, A_STREAMS, HEAD_DIM, A_GROUP * tq), BF16),
            pltpu.VMEM((n_kv, HEAD_DIM + ONES_ROWS, tk), BF16),
            pltpu.VMEM((2, A_STREAMS, tk, A_GROUP * tq), F32),
            pltpu.VMEM((A_STREAMS, 1, A_GROUP * tq), F32),
            pltpu.VMEM((A_STREAMS, HEAD_DIM + ONES_ROWS, A_GROUP * tq), F32),
        ],
        compiler_params=pltpu.CompilerParams(
            dimension_semantics=("parallel", "parallel", "arbitrary"),
            vmem_limit_bytes=VMEM_LIMIT_BYTES),
        name="attn_a",
    )(proj, proj, proj, proj, proj, proj, proj)


def _attn_b_kernel(q_ref, k_ref, v_ref, gate_ref, bias_ref, o_ref, vt_scr,
                   *, blocks_per_step, grid_rows):
    t = pl.program_id(2)
    bq = B_QROWS * GRID_W
    wk = B_WROWS * GRID_W
    n_blocks = grid_rows // B_QROWS

    @pl.when(t == 0)
    def _():
        _build_vt(v_ref, vt_scr, n_kv=n_blocks, tk=bq, dv=HEAD_DIM)

    def block_rows(u):
        return slice(u * bq, (u + 1) * bq)

    def window(u):
        blk = t * blocks_per_step + u
        win_row = jnp.clip(blk * B_QROWS - NA_ROWS // 2, 0, grid_rows - B_WROWS)
        case = jnp.where(blk == 0, 0, jnp.where(blk == n_blocks - 1, 2, 1))
        return win_row, case

    def scores(u):
        win_row, case = window(u)
        qt = q_ref[block_rows(u), :].astype(F32).T.astype(BF16)
        k = k_ref[pl.ds(pl.multiple_of(win_row * GRID_W, bq), wk), :]
        return jnp.dot(k, qt, preferred_element_type=F32) + bias_ref[case]

    def softmax(s):
        return jnp.exp2(s - jnp.max(s, axis=0, keepdims=True)).astype(BF16)

    def weighted_values(u, p):
        chunk0 = window(u)[0] // B_QROWS
        vt = jnp.concatenate([vt_scr[chunk0 + w] for w in range(B_WROWS // B_QROWS)], axis=1)
        return jnp.dot(vt, p, preferred_element_type=F32)

    def store(u, acc):
        o = (acc[0:HEAD_DIM] / acc[HEAD_DIM:HEAD_DIM + 1]).T
        gate = gate_ref[block_rows(u), :].astype(F32)
        o_ref[block_rows(u), :] = (o * _silu(gate)).astype(BF16)

    n = blocks_per_step
    s, p, acc = {0: scores(0)}, {}, {}
    for u in range(n + 2):
        if u + 1 < n:
            s[u + 1] = scores(u + 1)
        if 0 <= u - 1 < n:
            acc[u - 1] = weighted_values(u - 1, p.pop(u - 1))
        if u < n:
            p[u] = softmax(s.pop(u))
        if 0 <= u - 2 < n:
            store(u - 2, acc.pop(u - 2))


def _attn_b(proj, bias_blocks, layer, batch, seq_len):
    tq = min(B_TQ, seq_len)
    bq = B_QROWS * GRID_W
    assert seq_len % tq == 0 and tq % bq == 0
    grid_rows = seq_len // GRID_W
    assert grid_rows % B_QROWS == 0 and grid_rows >= 4 * B_QROWS
    nq = seq_len // tq
    n_blocks = grid_rows // B_QROWS
    kernel = functools.partial(_attn_b_kernel, blocks_per_step=tq // bq, grid_rows=grid_rows)
    return pl.pallas_call(
        kernel,
        grid=(batch, B_HEADS, nq),
        in_specs=[
            pl.BlockSpec((tq, HEAD_DIM), lambda b, h, i: (b * nq + i, OFF_QB // HEAD_DIM + h)),
            pl.BlockSpec((seq_len, HEAD_DIM), lambda b, h, i: (b, OFF_KB // HEAD_DIM + h)),
            pl.BlockSpec((seq_len, HEAD_DIM), lambda b, h, i: (b, OFF_VB // HEAD_DIM + h)),
            pl.BlockSpec((tq, HEAD_DIM),
                         lambda b, h, i: (b * nq + i, (OFF_GATE + A_Q) // HEAD_DIM + h)),
            pl.BlockSpec((None, None, B_CASES, B_WROWS * GRID_W, bq),
                         lambda b, h, i: (layer, h, 0, 0, 0)),
        ],
        out_specs=pl.BlockSpec((tq, HEAD_DIM), lambda b, h, i: (b * nq + i, h)),
        out_shape=jax.ShapeDtypeStruct((batch * seq_len, B_W), BF16),
        scratch_shapes=[pltpu.VMEM((n_blocks, HEAD_DIM + ONES_ROWS, bq), BF16)],
        compiler_params=pltpu.CompilerParams(
            dimension_semantics=("parallel", "parallel", "arbitrary"),
            vmem_limit_bytes=VMEM_LIMIT_BYTES),
        name="attn_b",
    )(proj, proj, proj, proj, bias_blocks)


def _bias_blocks(rel_bias):
    layers, heads = rel_bias.shape[:2]
    n_dr, n_dc = 2 * NA_ROWS - 1, 2 * NA_COLS - 1
    c = np.arange(GRID_W)
    col_start = np.clip(c - NA_COLS // 2, 0, GRID_W - NA_COLS)
    col_ok = (c[None, :] >= col_start[:, None]) & (c[None, :] < col_start[:, None] + NA_COLS)
    dc = np.clip(c[None, :] - c[:, None], -(NA_COLS - 1), NA_COLS - 1) + (NA_COLS - 1)
    any_rows = 8 * B_QROWS
    pick_dr, row_ok = [], []
    for first_row, win_row in ((0, 0), (2 * B_QROWS, B_QROWS), (any_rows - B_QROWS,
                                                                 any_rows - B_WROWS)):
        r = first_row + np.arange(B_QROWS)
        kr = win_row + np.arange(B_WROWS)
        band0 = np.clip(r - NA_ROWS // 2, 0, any_rows - NA_ROWS)
        row_ok.append((kr[None, :] >= band0[:, None]) & (kr[None, :] < band0[:, None] + NA_ROWS))
        dr = np.clip(kr[None, :] - r[:, None] + (NA_ROWS - 1), 0, n_dr - 1)
        pick_dr.append(dr.reshape(-1, 1) == np.arange(n_dr)[None, :])
    pick_dr = np.stack(pick_dr).astype(np.float32)
    pick_dc = (dc.reshape(-1, 1) == np.arange(n_dc)[None, :]).astype(np.float32)
    bias = jnp.einsum("lhij,cpi,qj->lhcpq", rel_bias.astype(F32), pick_dr, pick_dc,
                      precision=lax.Precision.HIGHEST)
    bias = bias.reshape(layers, heads, B_CASES, B_QROWS, B_WROWS, GRID_W, GRID_W)
    ok = np.stack(row_ok)[:, :, :, None, None] & col_ok[None, None, None, :, :]
    bias = jnp.where(ok[None, None], bias * math.log2(math.e), NEG_INF)
    return jnp.transpose(bias, (0, 1, 2, 4, 6, 3, 5)).reshape(
        layers, heads, B_CASES, B_WROWS * GRID_W, B_QROWS * GRID_W)


def _attn_c_kernel(q_ref, qn_ref, k_ref, v_ref, gate_ref, lq1_ref, lk1_ref, lq2_ref, lk2_ref,
                   sg_ref, o_ref, qt_scr, vt_scr, s_scr, m1_scr, a1_scr, m2_scr, a2_scr,
                   *, tk, n_kv, lam_init):
    dv = 2 * HEAD_DIM
    streams = ((m1_scr, a1_scr), (m2_scr, a2_scr))
    i = pl.program_id(2)
    cur = i % 2
    assert n_kv % 2 == 0

    def load_qt(ref, slot, u):
        qt_scr[slot, u] = ref[:, u * HEAD_DIM:(u + 1) * HEAD_DIM].astype(F32).T.astype(BF16)

    def scores(j, slot, u):
        start = pl.multiple_of(j * tk, tk)
        k = k_ref[pl.ds(start, tk), u * HEAD_DIM:(u + 1) * HEAD_DIM]
        return jnp.dot(k, qt_scr[slot, u], preferred_element_type=F32)

    @pl.when(i == 0)
    def _():
        _build_vt(v_ref, vt_scr, n_kv=n_kv, tk=tk, dv=dv)
        for u in range(2):
            load_qt(q_ref, 0, u)
            s_scr[0, u] = scores(0, 0, u)

    for m_scr, a_scr in streams:
        m_scr[...] = jnp.full(m_scr.shape, NEG_INF, F32)
        a_scr[...] = jnp.zeros(a_scr.shape, F32)

    unroll = _kv_unroll(n_kv, KV_UNROLL // 2)

    def step_group(jj, is_tail):
        for t in range(unroll):
            j = unroll * jj + t
            slot = t % 2
            for u, (m_scr, a_scr) in enumerate(streams):
                if is_tail and t == unroll - 1:
                    load_qt(qn_ref, 1 - cur, u)
                    s_scr[0, u] = scores(0, 1 - cur, u)
                else:
                    s_scr[1 - slot, u] = scores(j + 1, cur, u)
                _flash_step(s_scr[slot, u], vt_scr[j], m_scr, a_scr)

    def body(jj, carry):
        step_group(jj, False)
        return carry

    lax.fori_loop(0, n_kv // unroll - 1, body, 0)
    step_group(n_kv // unroll - 1, True)

    lam = (jnp.exp(jnp.sum(lq1_ref[...] * lk1_ref[...], axis=-1, keepdims=True))
           - jnp.exp(jnp.sum(lq2_ref[...] * lk2_ref[...], axis=-1, keepdims=True))
           + lam_init)
    a1 = a1_scr[...]
    a2 = a2_scr[...]
    o = (a1[0:dv] / a1[dv:dv + 1] - lam * (a2[0:dv] / a2[dv:dv + 1])).T
    ms = jnp.sum(o * o, axis=-1, keepdims=True) * (1.0 / (2 * HEAD_DIM))
    o = (o * lax.rsqrt(ms + EPS)) * sg_ref[...] * (1.0 - lam_init)
    o_ref[...] = (o * _silu(gate_ref[...].astype(F32))).astype(BF16)


def _attn_c(proj, lq1, lk1, lq2, lk2, subln_g, lam_init, batch, seq_len):
    tq, tk = C_TQ, C_TK
    assert seq_len % tq == 0 and seq_len % tk == 0
    nq = seq_len // tq
    n_kv = seq_len // tk
    hw = 2 * HEAD_DIM
    kernel = functools.partial(_attn_c_kernel, tk=tk, n_kv=n_kv, lam_init=lam_init)
    vec = pl.BlockSpec((1, HEAD_DIM), lambda b, h, i: (0, 0))
    return pl.pallas_call(
        kernel,
        grid=(batch, C_HEADS, nq),
        in_specs=[
            pl.BlockSpec((tq, hw), lambda b, h, i: (b * nq + i, OFF_QC // hw + h)),
            pl.BlockSpec((tq, hw),
                         lambda b, h, i: (b * nq + jnp.minimum(i + 1, nq - 1), OFF_QC // hw + h)),
            pl.BlockSpec((seq_len, hw), lambda b, h, i: (b, OFF_KC // hw + h)),
            pl.BlockSpec((seq_len, hw), lambda b, h, i: (b, OFF_VC // hw + h)),
            pl.BlockSpec((tq, hw), lambda b, h, i: (b * nq + i, (OFF_GATE + A_Q + B_W) // hw + h)),
            vec, vec, vec, vec,
            pl.BlockSpec((1, hw), lambda b, h, i: (0, 0)),
        ],
        out_specs=pl.BlockSpec((tq, hw), lambda b, h, i: (b * nq + i, h)),
        out_shape=jax.ShapeDtypeStruct((batch * seq_len, C_V), BF16),
        scratch_shapes=[
            pltpu.VMEM((2, 2, HEAD_DIM, tq), BF16),
            pltpu.VMEM((n_kv, hw + ONES_ROWS, tk), BF16),
            pltpu.VMEM((2, 2, tk, tq), F32),
            pltpu.VMEM((1, tq), F32), pltpu.VMEM((hw + ONES_ROWS, tq), F32),
            pltpu.VMEM((1, tq), F32), pltpu.VMEM((hw + ONES_ROWS, tq), F32),
        ],
        compiler_params=pltpu.CompilerParams(
            dimension_semantics=("parallel", "parallel", "arbitrary"),
            vmem_limit_bytes=VMEM_LIMIT_BYTES),
        name="attn_c",
    )(proj, proj, proj, proj, proj, lq1.reshape(1, HEAD_DIM), lk1.reshape(1, HEAD_DIM),
      lq2.reshape(1, HEAD_DIM), lk2.reshape(1, HEAD_DIM), subln_g.reshape(1, hw))


def _out_proj_kernel(a_ref, b_ref, c_ref, *rest, n_row_tiles, n_col_tiles):
    n_chunks = D_MIX // OUT_WROWS
    w_refs = rest[:n_chunks]
    x_ref, g_ref, o_ref, y_scr, ss_scr = rest[n_chunks:]
    i = pl.program_id(0)
    j = pl.program_id(1)
    cur = i % 2
    prev = 1 - cur

    def matmul():
        y = None
        chunk = 0
        for ref in (a_ref, b_ref, c_ref):
            for off in range(0, ref.shape[1], OUT_WROWS):
                part = jnp.dot(ref[:, off:off + OUT_WROWS], w_refs[chunk][...],
                               preferred_element_type=F32)
                y = part if y is None else y + part
                chunk += 1
        y_scr[cur, j] = y
        ss_scr[cur, j] = jnp.sum(y * y, axis=-1, keepdims=True)

    def finish():
        ss = ss_scr[prev, 0]
        for jj in range(1, n_col_tiles):
            ss = ss + ss_scr[prev, jj]
        inv = lax.rsqrt(ss * (1.0 / D_MODEL) + EPS)
        o_ref[...] = x_ref[...] + (y_scr[prev, j] * inv) * g_ref[...]

    pl.when(i == 0)(matmul)

    @pl.when(jnp.logical_and(i > 0, i < n_row_tiles))
    def _():
        finish()
        matmul()

    pl.when(i == n_row_tiles)(finish)


def _out_proj(mix_a, mix_b, mix_c, w_out_bf, layer, x2d, post_g):
    tokens = x2d.shape[0]
    tm, tn = OUT_TM, OUT_TN
    assert tokens % tm == 0 and D_MODEL % tn == 0
    n_row_tiles = tokens // tm
    n_col_tiles = D_MODEL // tn
    kernel = functools.partial(_out_proj_kernel, n_row_tiles=n_row_tiles, n_col_tiles=n_col_tiles)

    def lhs_row(i, j):
        return (jnp.minimum(i, n_row_tiles - 1), 0)

    def done_tile(i, j):
        return (jnp.maximum(i - 1, 0), jnp.where(i == 0, 0, j))

    def w_spec(r):
        return pl.BlockSpec((None, OUT_WROWS, tn), lambda i, j: (layer, r, j))

    assert A_Q % OUT_WROWS == 0 and B_W % OUT_WROWS == 0 and C_V % OUT_WROWS == 0
    return pl.pallas_call(
        kernel,
        grid=(n_row_tiles + 1, n_col_tiles),
        in_specs=[
            pl.BlockSpec((tm, A_Q), lhs_row),
            pl.BlockSpec((tm, B_W), lhs_row),
            pl.BlockSpec((tm, C_V), lhs_row),
            *[w_spec(r) for r in range(D_MIX // OUT_WROWS)],
            pl.BlockSpec((tm, tn), done_tile),
            pl.BlockSpec((1, tn), lambda i, j: (0, j)),
        ],
        out_specs=pl.BlockSpec((tm, tn), done_tile),
        out_shape=jax.ShapeDtypeStruct((tokens, D_MODEL), F32),
        scratch_shapes=[pltpu.VMEM((2, n_col_tiles, tm, tn), F32),
                        pltpu.VMEM((2, n_col_tiles, tm, 1), F32)],
        compiler_params=pltpu.CompilerParams(
            dimension_semantics=("arbitrary", "arbitrary"),
            vmem_limit_bytes=VMEM_LIMIT_BYTES),
        name="out_proj",
    )(mix_a, mix_b, mix_c, *([w_out_bf] * (D_MIX // OUT_WROWS)), x2d,
      post_g.reshape(1, D_MODEL))


def _angles(pos, dim, theta):
    inv = jnp.power(theta, -jnp.arange(0, dim, 2, dtype=F32) / dim)
    return pos.astype(F32)[:, None] * inv[None, :]


def _rope_tables(seq_len):
    t = jnp.arange(seq_len)
    ang_row = _angles(t // GRID_W, HEAD_DIM // 2, AXIAL_THETA)
    ang_col = _angles(t % GRID_W, HEAD_DIM // 2, AXIAL_THETA)
    cr, sr, cc_, sc_ = jnp.cos(ang_row), jnp.sin(ang_row), jnp.cos(ang_col), jnp.sin(ang_col)
    cos_ax = jnp.concatenate([cr, cr, cc_, cc_], axis=-1)
    sin_ax = jnp.concatenate([-sr, sr, -sc_, sc_], axis=-1)
    ang_t = _angles(t, ROPE_DIMS, ROPE_THETA)
    ct, st = jnp.cos(ang_t), jnp.sin(ang_t)
    rest = HEAD_DIM - ROPE_DIMS
    cos_p = jnp.concatenate([ct, ct, jnp.ones((seq_len, rest), F32)], axis=-1)
    sin_p = jnp.concatenate([-st, st, jnp.zeros((seq_len, rest), F32)], axis=-1)
    return cos_ax, sin_ax, cos_p, sin_p


def _trunk(x, params, w_in_bf, w_out_bf, bias_blocks):
    batch, seq_len, _ = x.shape
    tables = _rope_tables(seq_len)
    x2d = x.reshape(batch * seq_len, D_MODEL)
    depth = w_in_bf.shape[0]
    for l in range(depth):
        lam_init = 0.8 - 0.6 * math.exp(-0.3 * l)
        proj = _in_proj(x2d, params["pre_norm_g"][l], w_in_bf, l, params["a_q_norm_g"][l],
                        params["a_k_norm_g"][l], tables, seq_len)
        mix_a = _attn_a(proj, batch, seq_len)
        mix_b = _attn_b(proj, bias_blocks, l, batch, seq_len)
        mix_c = _attn_c(proj, params["c_lambda_q1"][l], params["c_lambda_k1"][l],
                        params["c_lambda_q2"][l], params["c_lambda_k2"][l],
                        params["c_subln_g"][l], lam_init, batch, seq_len)
        x2d = _out_proj(mix_a, mix_b, mix_c, w_out_bf, l, x2d, params["post_norm_g"][l])
    return x2d.reshape(batch, seq_len, D_MODEL)


def _prepare(params):
    w_in_bf = params["w_in"].astype(BF16)
    w_out_bf = params["w_out"].astype(BF16)
    return w_in_bf, w_out_bf, _bias_blocks(params["b_rel_bias"])


def kernel(x_prompt, x_sample, pre_norm_g, post_norm_g, w_in, w_out, a_q_norm_g, a_k_norm_g,
           b_rel_bias, c_lambda_q1, c_lambda_k1, c_lambda_q2, c_lambda_k2, c_subln_g):
    params = dict(pre_norm_g=pre_norm_g, post_norm_g=post_norm_g, a_q_norm_g=a_q_norm_g,
                  a_k_norm_g=a_k_norm_g, c_lambda_q1=c_lambda_q1, c_lambda_k1=c_lambda_k1,
                  c_lambda_q2=c_lambda_q2, c_lambda_k2=c_lambda_k2, c_subln_g=c_subln_g)
    params.update(w_in=w_in, w_out=w_out, b_rel_bias=b_rel_bias)
    prepared = _prepare(params)
    y_prompt = _trunk(x_prompt, params, *prepared)
    y_sample = _trunk(x_sample, params, *prepared)
    return (y_prompt, y_sample)
```

```python
import functools
import math

import jax
import jax.numpy as jnp
import numpy as np
from jax import lax
from jax.experimental import pallas as pl
from jax.experimental.pallas import tpu as pltpu

F32 = jnp.float32
BF16 = jnp.bfloat16

D_MODEL = 4096
HEAD_DIM = 128
GRID_W = 64
A_HEADS = 12
A_KV_HEADS = 4
A_GROUP = A_HEADS // A_KV_HEADS
B_HEADS = 8
C_HEADS = 6
A_Q = A_HEADS * HEAD_DIM
A_KV = A_KV_HEADS * HEAD_DIM
B_W = B_HEADS * HEAD_DIM
C_QK = C_HEADS * 2 * HEAD_DIM
C_V = C_HEADS * 2 * HEAD_DIM
D_MIX = A_Q + B_W + C_V
D_IN = A_Q + 2 * A_KV + 3 * B_W + 2 * C_QK + C_V + D_MIX
NA_ROWS = 8
NA_COLS = 16
AXIAL_THETA = 10000.0
ROPE_THETA = 500000.0
ROPE_DIMS = HEAD_DIM // 4
EPS = 1e-6
NEG_INF = -1e30
SCALE = 1.0 / math.sqrt(HEAD_DIM)
SCALE_LOG2 = SCALE * math.log2(math.e)
ONES_ROWS = 16

OFF_QA = 0
OFF_KA = OFF_QA + A_Q
OFF_VA = OFF_KA + A_KV
OFF_QB = OFF_VA + A_KV
OFF_KB = OFF_QB + B_W
OFF_VB = OFF_KB + B_W
OFF_QC = OFF_VB + B_W
OFF_KC = OFF_QC + C_QK
OFF_VC = OFF_KC + C_QK
OFF_GATE = OFF_VC + C_V

VMEM_LIMIT_BYTES = 56 * 1024 * 1024

IN_TM = 512
IN_TN = 512
IN_SEG = 512
IN_KC = 256
OUT_TM = 512
OUT_TN = 1024
A_TQ = 256
A_STREAMS = 2
A_TK = 512
B_TQ = 2048
B_QROWS = 4
B_WROWS = B_QROWS + NA_ROWS
B_CASES = 3
C_TQ = 512
C_TK = 512
KV_UNROLL = 16


def _swap_halves(x, h):
    lane = lax.broadcasted_iota(jnp.int32, x.shape, 1)
    first = (lane % (2 * h)) < h
    return jnp.where(first, pltpu.roll(x, HEAD_DIM - h, 1), pltpu.roll(x, h, 1))


def _silu(g):
    return g / (1.0 + jnp.exp(-g))


def _segment_kind(off):
    if off < OFF_KA:
        return "qa"
    if off < OFF_VA:
        return "ka"
    if OFF_QB <= off < OFF_KB:
        return "qb"
    if OFF_QC <= off < OFF_KC:
        return "qc"
    if OFF_KC <= off < OFF_VC:
        return "kc"
    return "plain"


def _in_tile_kinds(n_tiles):
    segs = IN_TN // IN_SEG
    return [tuple(_segment_kind((t * segs + s) * IN_SEG) for s in range(segs))
            for t in range(n_tiles)]


def _in_proj_kernel(x_ref, g_ref, w_ref, qn_ref, kn_ref, ca_ref, sa_ref, cc_ref, sc_ref,
                    o_ref, h_scr, acc_scr, *, n_tiles):
    j = pl.program_id(1)
    cur = j % 2
    prev = 1 - cur
    heads = IN_SEG // HEAD_DIM

    def pre_norm():
        x = x_ref[...]
        ms = jnp.sum(x * x, axis=-1, keepdims=True) * (1.0 / D_MODEL)
        h = ((x * lax.rsqrt(ms + EPS)) * g_ref[...]).astype(BF16)
        for k in range(D_MODEL // IN_KC):
            h_scr[k] = h[:, k * IN_KC:(k + 1) * IN_KC]

    def matmul():
        acc = None
        for k in range(D_MODEL // IN_KC):
            part = jnp.dot(h_scr[k], w_ref[k * IN_KC:(k + 1) * IN_KC, :],
                           preferred_element_type=F32)
            acc = part if acc is None else acc + part
        acc_scr[cur] = acc

    def first_step():
        pre_norm()
        matmul()

    def head_norm(y, g):
        ms = jnp.sum(y * y, axis=-1, keepdims=True) * (1.0 / HEAD_DIM)
        return (y * lax.rsqrt(ms + EPS)) * g

    def axial(y):
        return y * ca_ref[...] + _swap_halves(y, HEAD_DIM // 4) * sa_ref[...]

    def partial(y):
        return y * cc_ref[...] + _swap_halves(y, ROPE_DIMS // 2) * sc_ref[...]

    epilogues = {
        "qa": lambda y: axial(head_norm(y, qn_ref[...])) * SCALE_LOG2,
        "ka": lambda y: axial(head_norm(y, kn_ref[...])),
        "qb": lambda y: y * SCALE_LOG2,
        "qc": lambda y: partial(y) * SCALE_LOG2,
        "kc": partial,
        "plain": lambda y: y,
    }

    def finish(kinds):
        for seg, kind in enumerate(kinds):
            for hh in range(heads):
                lo = seg * IN_SEG + hh * HEAD_DIM
                o_ref[:, lo:lo + HEAD_DIM] = epilogues[kind](
                    acc_scr[prev, :, lo:lo + HEAD_DIM]).astype(BF16)

    def step(kinds):
        finish(kinds)
        matmul()

    tile_kinds = _in_tile_kinds(n_tiles)
    pl.when(j == 0)(first_step)
    for kinds in sorted(set(tile_kinds[:-1])):
        tiles = [t for t in range(n_tiles - 1) if tile_kinds[t] == kinds]
        cond = functools.reduce(jnp.logical_or, [j == t + 1 for t in tiles])
        pl.when(cond)(functools.partial(step, kinds))
    pl.when(j == n_tiles)(functools.partial(finish, tile_kinds[-1]))


def _in_proj(x2d, pre_g, w_in_bf, layer, qn_g, kn_g, tables, seq_len):
    tokens = x2d.shape[0]
    tm = IN_TM
    assert tokens % tm == 0 and seq_len % tm == 0 and D_IN % IN_TN == 0 and IN_TN % IN_SEG == 0
    for off in (OFF_KA, OFF_VA, OFF_QB, OFF_KB, OFF_QC, OFF_KC, OFF_VC):
        assert off % IN_SEG == 0
    n_tiles = D_IN // IN_TN
    pos_blocks = seq_len // tm
    tab_spec = pl.BlockSpec((tm, HEAD_DIM), lambda i, j: (i % pos_blocks, 0))
    vec_spec = pl.BlockSpec((1, HEAD_DIM), lambda i, j: (0, 0))
    return pl.pallas_call(
        functools.partial(_in_proj_kernel, n_tiles=n_tiles),
        grid=(tokens // tm, n_tiles + 1),
        in_specs=[
            pl.BlockSpec((tm, D_MODEL), lambda i, j: (i, 0)),
            pl.BlockSpec((1, D_MODEL), lambda i, j: (0, 0)),
            pl.BlockSpec((None, D_MODEL, IN_TN),
                         lambda i, j: (layer, 0, jnp.minimum(j, n_tiles - 1))),
            vec_spec, vec_spec, tab_spec, tab_spec, tab_spec, tab_spec,
        ],
        out_specs=pl.BlockSpec((tm, IN_TN), lambda i, j: (i, jnp.maximum(j - 1, 0))),
        out_shape=jax.ShapeDtypeStruct((tokens, D_IN), BF16),
        scratch_shapes=[pltpu.VMEM((D_MODEL // IN_KC, tm, IN_KC), BF16),
                        pltpu.VMEM((2, tm, IN_TN), F32)],
        compiler_params=pltpu.CompilerParams(
            dimension_semantics=("parallel", "arbitrary"),
            vmem_limit_bytes=VMEM_LIMIT_BYTES),
        name="in_proj",
    )(x2d, pre_g.reshape(1, D_MODEL), w_in_bf, qn_g.reshape(1, HEAD_DIM),
      kn_g.reshape(1, HEAD_DIM), *tables)


def _build_vt(v_ref, vt_scr, *, n_kv, tk, dv):
    def chunk(c, carry):
        off = pl.multiple_of(c * tk, tk)
        vt_scr[c, 0:dv, :] = v_ref[pl.ds(off, tk), :].astype(F32).T.astype(BF16)
        row = lax.broadcasted_iota(jnp.int32, (ONES_ROWS, tk), 0)
        vt_scr[c, dv:dv + ONES_ROWS, :] = jnp.where(row == 0, 1.0, 0.0).astype(BF16)
        return carry

    lax.fori_loop(0, n_kv, chunk, 0)


def _kv_unroll(n_kv, cap):
    unroll = min(cap, n_kv)
    assert unroll % 2 == 0 and n_kv % unroll == 0
    return unroll


def _flash_step(s, vt, m_scr, acc_scr):
    m_old = m_scr[...]
    m_new = jnp.maximum(m_old, jnp.max(s, axis=0, keepdims=True))
    alpha = jnp.exp2(m_old - m_new)
    p = jnp.exp2(s - m_new).astype(BF16)
    acc_scr[...] = alpha * acc_scr[...] + jnp.dot(vt, p, preferred_element_type=F32)
    m_scr[...] = m_new


def _attn_a_kernel(q_ref, qn_ref, k_ref, v_ref, g0_ref, g1_ref, g2_ref, o_ref,
                   qt_scr, vt_scr, s_scr, m_scr, acc_scr, *, tq, tk, n_kv):
    gate_refs = (g0_ref, g1_ref, g2_ref)
    i = pl.program_id(2)
    cur = i % 2
    assert n_kv % 2 == 0
    streams = range(A_STREAMS)

    def load_qt(ref, slot, st):
        for g in range(A_GROUP):
            qt_scr[slot, st, :, g * tq:(g + 1) * tq] = (
                ref[st * tq:(st + 1) * tq,
                    g * HEAD_DIM:(g + 1) * HEAD_DIM].astype(F32).T.astype(BF16))

    def scores(j, slot, st):
        start = pl.multiple_of(j * tk, tk)
        return jnp.dot(k_ref[pl.ds(start, tk), :], qt_scr[slot, st],
                       preferred_element_type=F32)

    @pl.when(i == 0)
    def _():
        _build_vt(v_ref, vt_scr, n_kv=n_kv, tk=tk, dv=HEAD_DIM)
        for st in streams:
            load_qt(q_ref, 0, st)
            s_scr[0, st] = scores(0, 0, st)

    m_scr[...] = jnp.full(m_scr.shape, NEG_INF, F32)
    acc_scr[...] = jnp.zeros(acc_scr.shape, F32)

    unroll = _kv_unroll(n_kv, KV_UNROLL // A_STREAMS)

    def step_group(jj, is_tail):
        for u in range(unroll):
            j = unroll * jj + u
            slot = u % 2
            for st in streams:
                if is_tail and u == unroll - 1:
                    load_qt(qn_ref, 1 - cur, st)
                    s_scr[0, st] = scores(0, 1 - cur, st)
                else:
                    s_scr[1 - slot, st] = scores(j + 1, cur, st)
                _flash_step(s_scr[slot, st], vt_scr[j], m_scr.at[st], acc_scr.at[st])

    def body(jj, carry):
        step_group(jj, False)
        return carry

    lax.fori_loop(0, n_kv // unroll - 1, body, 0)
    step_group(n_kv // unroll - 1, True)
    for st in streams:
        acc = acc_scr[st]
        out = (acc[0:HEAD_DIM] / acc[HEAD_DIM:HEAD_DIM + 1]).T
        rows = slice(st * tq, (st + 1) * tq)
        for g in range(A_GROUP):
            sl = slice(g * HEAD_DIM, (g + 1) * HEAD_DIM)
            gate = gate_refs[g][rows, :].astype(F32)
            o_ref[rows, sl] = (out[g * tq:(g + 1) * tq, :] * _silu(gate)).astype(BF16)


def _attn_a(proj, batch, seq_len):
    tq, tk = A_TQ, A_TK
    tile = A_STREAMS * tq
    assert seq_len % tile == 0 and seq_len % tk == 0
    nq = seq_len // tile
    n_kv = seq_len // tk
    gw = A_GROUP * HEAD_DIM
    kernel = functools.partial(_attn_a_kernel, tq=tq, tk=tk, n_kv=n_kv)

    def gate_spec(g):
        return pl.BlockSpec(
            (tile, HEAD_DIM),
            lambda b, h, i: (b * nq + i, OFF_GATE // HEAD_DIM + A_GROUP * h + g))

    return pl.pallas_call(
        kernel,
        grid=(batch, A_KV_HEADS, nq),
        in_specs=[
            pl.BlockSpec((tile, gw), lambda b, h, i: (b * nq + i, h)),
            pl.BlockSpec((tile, gw), lambda b, h, i: (b * nq + jnp.minimum(i + 1, nq - 1), h)),
            pl.BlockSpec((seq_len, HEAD_DIM), lambda b, h, i: (b, OFF_KA // HEAD_DIM + h)),
            pl.BlockSpec((seq_len, HEAD_DIM), lambda b, h, i: (b, OFF_VA // HEAD_DIM + h)),
            gate_spec(0), gate_spec(1), gate_spec(2),
        ],
        out_specs=pl.BlockSpec((tile, gw), lambda b, h, i: (b * nq + i, h)),
        out_shape=jax.ShapeDtypeStruct((batch * seq_len, A_Q), BF16),
        scratch_shapes=[
            pltpu.VMEM((2, A_STREAMS, HEAD_DIM, A_GROUP * tq), BF16),
            pltpu.VMEM((n_kv, HEAD_DIM + ONES_ROWS, tk), BF16),
            pltpu.VMEM((2, A_STREAMS, tk, A_GROUP * tq), F32),
            pltpu.VMEM((A_STREAMS, 1, A_GROUP * tq), F32),
            pltpu.VMEM((A_STREAMS, HEAD_DIM + ONES_ROWS, A_GROUP * tq), F32),
        ],
        compiler_params=pltpu.CompilerParams(
            dimension_semantics=("parallel", "parallel", "arbitrary"),
            vmem_limit_bytes=VMEM_LIMIT_BYTES),
        name="attn_a",
    )(proj, proj, proj, proj, proj, proj, proj)


def _attn_b_kernel(q_ref, k_ref, v_ref, gate_ref, bias_ref, o_ref, vt_scr,
                   *, blocks_per_step, grid_rows):
    t = pl.program_id(2)
    bq = B_QROWS * GRID_W
    wk = B_WROWS * GRID_W
    n_blocks = grid_rows // B_QROWS

    @pl.when(t == 0)
    def _():
        _build_vt(v_ref, vt_scr, n_kv=n_blocks, tk=bq, dv=HEAD_DIM)

    def block_rows(u):
        return slice(u * bq, (u + 1) * bq)

    def window(u):
        blk = t * blocks_per_step + u
        win_row = jnp.clip(blk * B_QROWS - NA_ROWS // 2, 0, grid_rows - B_WROWS)
        case = jnp.where(blk == 0, 0, jnp.where(blk == n_blocks - 1, 2, 1))
        return win_row, case

    def scores(u):
        win_row, case = window(u)
        qt = q_ref[block_rows(u), :].astype(F32).T.astype(BF16)
        k = k_ref[pl.ds(pl.multiple_of(win_row * GRID_W, bq), wk), :]
        return jnp.dot(k, qt, preferred_element_type=F32) + bias_ref[case]

    def softmax(s):
        return jnp.exp2(s - jnp.max(s, axis=0, keepdims=True)).astype(BF16)

    def weighted_values(u, p):
        chunk0 = window(u)[0] // B_QROWS
        vt = jnp.concatenate([vt_scr[chunk0 + w] for w in range(B_WROWS // B_QROWS)], axis=1)
        return jnp.dot(vt, p, preferred_element_type=F32)

    def store(u, acc):
        o = (acc[0:HEAD_DIM] / acc[HEAD_DIM:HEAD_DIM + 1]).T
        gate = gate_ref[block_rows(u), :].astype(F32)
        o_ref[block_rows(u), :] = (o * _silu(gate)).astype(BF16)

    n = blocks_per_step
    s, p, acc = {0: scores(0)}, {}, {}
    for u in range(n + 2):
        if u + 1 < n:
            s[u + 1] = scores(u + 1)
        if 0 <= u - 1 < n:
            acc[u - 1] = weighted_values(u - 1, p.pop(u - 1))
        if u < n:
            p[u] = softmax(s.pop(u))
        if 0 <= u - 2 < n:
            store(u - 2, acc.pop(u - 2))


def _attn_b(proj, bias_blocks, layer, batch, seq_len):
    tq = min(B_TQ, seq_len)
    bq = B_QROWS * GRID_W
    assert seq_len % tq == 0 and tq % bq == 0
    grid_rows = seq_len // GRID_W
    assert grid_rows % B_QROWS == 0 and grid_rows >= 4 * B_QROWS
    nq = seq_len // tq
    n_blocks = grid_rows // B_QROWS
    kernel = functools.partial(_attn_b_kernel, blocks_per_step=tq // bq, grid_rows=grid_rows)
    return pl.pallas_call(
        kernel,
        grid=(batch, B_HEADS, nq),
        in_specs=[
            pl.BlockSpec((tq, HEAD_DIM), lambda b, h, i: (b * nq + i, OFF_QB // HEAD_DIM + h)),
            pl.BlockSpec((seq_len, HEAD_DIM), lambda b, h, i: (b, OFF_KB // HEAD_DIM + h)),
            pl.BlockSpec((seq_len, HEAD_DIM), lambda b, h, i: (b, OFF_VB // HEAD_DIM + h)),
            pl.BlockSpec((tq, HEAD_DIM),
                         lambda b, h, i: (b * nq + i, (OFF_GATE + A_Q) // HEAD_DIM + h)),
            pl.BlockSpec((None, None, B_CASES, B_WROWS * GRID_W, bq),
                         lambda b, h, i: (layer, h, 0, 0, 0)),
        ],
        out_specs=pl.BlockSpec((tq, HEAD_DIM), lambda b, h, i: (b * nq + i, h)),
        out_shape=jax.ShapeDtypeStruct((batch * seq_len, B_W), BF16),
        scratch_shapes=[pltpu.VMEM((n_blocks, HEAD_DIM + ONES_ROWS, bq), BF16)],
        compiler_params=pltpu.CompilerParams(
            dimension_semantics=("parallel", "parallel", "arbitrary"),
            vmem_limit_bytes=VMEM_LIMIT_BYTES),
        name="attn_b",
    )(proj, proj, proj, proj, bias_blocks)


def _bias_blocks(rel_bias):
    layers, heads = rel_bias.shape[:2]
    n_dr, n_dc = 2 * NA_ROWS - 1, 2 * NA_COLS - 1
    c = np.arange(GRID_W)
    col_start = np.clip(c - NA_COLS // 2, 0, GRID_W - NA_COLS)
    col_ok = (c[None, :] >= col_start[:, None]) & (c[None, :] < col_start[:, None] + NA_COLS)
    dc = np.clip(c[None, :] - c[:, None], -(NA_COLS - 1), NA_COLS - 1) + (NA_COLS - 1)
    any_rows = 8 * B_QROWS
    pick_dr, row_ok = [], []
    for first_row, win_row in ((0, 0), (2 * B_QROWS, B_QROWS), (any_rows - B_QROWS,
                                                                 any_rows - B_WROWS)):
        r = first_row + np.arange(B_QROWS)
        kr = win_row + np.arange(B_WROWS)
        band0 = np.clip(r - NA_ROWS // 2, 0, any_rows - NA_ROWS)
        row_ok.append((kr[None, :] >= band0[:, None]) & (kr[None, :] < band0[:, None] + NA_ROWS))
        dr = np.clip(kr[None, :] - r[:, None] + (NA_ROWS - 1), 0, n_dr - 1)
        pick_dr.append(dr.reshape(-1, 1) == np.arange(n_dr)[None, :])
    pick_dr = np.stack(pick_dr).astype(np.float32)
    pick_dc = (dc.reshape(-1, 1) == np.arange(n_dc)[None, :]).astype(np.float32)
    bias = jnp.einsum("lhij,cpi,qj->lhcpq", rel_bias.astype(F32), pick_dr, pick_dc,
                      precision=lax.Precision.HIGHEST)
    bias = bias.reshape(layers, heads, B_CASES, B_QROWS, B_WROWS, GRID_W, GRID_W)
    ok = np.stack(row_ok)[:, :, :, None, None] & col_ok[None, None, None, :, :]
    bias = jnp.where(ok[None, None], bias * math.log2(math.e), NEG_INF)
    return jnp.transpose(bias, (0, 1, 2, 4, 6, 3, 5)).reshape(
        layers, heads, B_CASES, B_WROWS * GRID_W, B_QROWS * GRID_W)


def _attn_c_kernel(q_ref, qn_ref, k_ref, v_ref, gate_ref, lq1_ref, lk1_ref, lq2_ref, lk2_ref,
                   sg_ref, o_ref, qt_scr, vt_scr, s_scr, m1_scr, a1_scr, m2_scr, a2_scr,
                   *, tk, n_kv, lam_init):
    dv = 2 * HEAD_DIM
    streams = ((m1_scr, a1_scr), (m2_scr, a2_scr))
    i = pl.program_id(2)
    cur = i % 2
    assert n_kv % 2 == 0

    def load_qt(ref, slot, u):
        qt_scr[slot, u] = ref[:, u * HEAD_DIM:(u + 1) * HEAD_DIM].astype(F32).T.astype(BF16)

    def scores(j, slot, u):
        start = pl.multiple_of(j * tk, tk)
        k = k_ref[pl.ds(start, tk), u * HEAD_DIM:(u + 1) * HEAD_DIM]
        return jnp.dot(k, qt_scr[slot, u], preferred_element_type=F32)

    @pl.when(i == 0)
    def _():
        _build_vt(v_ref, vt_scr, n_kv=n_kv, tk=tk, dv=dv)
        for u in range(2):
            load_qt(q_ref, 0, u)
            s_scr[0, u] = scores(0, 0, u)

    for m_scr, a_scr in streams:
        m_scr[...] = jnp.full(m_scr.shape, NEG_INF, F32)
        a_scr[...] = jnp.zeros(a_scr.shape, F32)

    unroll = _kv_unroll(n_kv, KV_UNROLL // 2)

    def step_group(jj, is_tail):
        for t in range(unroll):
            j = unroll * jj + t
            slot = t % 2
            for u, (m_scr, a_scr) in enumerate(streams):
                if is_tail and t == unroll - 1:
                    load_qt(qn_ref, 1 - cur, u)
                    s_scr[0, u] = scores(0, 1 - cur, u)
                else:
                    s_scr[1 - slot, u] = scores(j + 1, cur, u)
                _flash_step(s_scr[slot, u], vt_scr[j], m_scr, a_scr)

    def body(jj, carry):
        step_group(jj, False)
        return carry

    lax.fori_loop(0, n_kv // unroll - 1, body, 0)
    step_group(n_kv // unroll - 1, True)

    lam = (jnp.exp(jnp.sum(lq1_ref[...] * lk1_ref[...], axis=-1, keepdims=True))
           - jnp.exp(jnp.sum(lq2_ref[...] * lk2_ref[...], axis=-1, keepdims=True))
           + lam_init)
    a1 = a1_scr[...]
    a2 = a2_scr[...]
    o = (a1[0:dv] / a1[dv:dv + 1] - lam * (a2[0:dv] / a2[dv:dv + 1])).T
    ms = jnp.sum(o * o, axis=-1, keepdims=True) * (1.0 / (2 * HEAD_DIM))
    o = (o * lax.rsqrt(ms + EPS)) * sg_ref[...] * (1.0 - lam_init)
    o_ref[...] = (o * _silu(gate_ref[...].astype(F32))).astype(BF16)


def _attn_c(proj, lq1, lk1, lq2, lk2, subln_g, lam_init, batch, seq_len):
    tq, tk = C_TQ, C_TK
    assert seq_len % tq == 0 and seq_len % tk == 0
    nq = seq_len // tq
    n_kv = seq_len // tk
    hw = 2 * HEAD_DIM
    kernel = functools.partial(_attn_c_kernel, tk=tk, n_kv=n_kv, lam_init=lam_init)
    vec = pl.BlockSpec((1, HEAD_DIM), lambda b, h, i: (0, 0))
    return pl.pallas_call(
        kernel,
        grid=(batch, C_HEADS, nq),
        in_specs=[
            pl.BlockSpec((tq, hw), lambda b, h, i: (b * nq + i, OFF_QC // hw + h)),
            pl.BlockSpec((tq, hw),
                         lambda b, h, i: (b * nq + jnp.minimum(i + 1, nq - 1), OFF_QC // hw + h)),
            pl.BlockSpec((seq_len, hw), lambda b, h, i: (b, OFF_KC // hw + h)),
            pl.BlockSpec((seq_len, hw), lambda b, h, i: (b, OFF_VC // hw + h)),
            pl.BlockSpec((tq, hw), lambda b, h, i: (b * nq + i, (OFF_GATE + A_Q + B_W) // hw + h)),
            vec, vec, vec, vec,
            pl.BlockSpec((1, hw), lambda b, h, i: (0, 0)),
        ],
        out_specs=pl.BlockSpec((tq, hw), lambda b, h, i: (b * nq + i, h)),
        out_shape=jax.ShapeDtypeStruct((batch * seq_len, C_V), BF16),
        scratch_shapes=[
            pltpu.VMEM((2, 2, HEAD_DIM, tq), BF16),
            pltpu.VMEM((n_kv, hw + ONES_ROWS, tk), BF16),
            pltpu.VMEM((2, 2, tk, tq), F32),
            pltpu.VMEM((1, tq), F32), pltpu.VMEM((hw + ONES_ROWS, tq), F32),
            pltpu.VMEM((1, tq), F32), pltpu.VMEM((hw + ONES_ROWS, tq), F32),
        ],
        compiler_params=pltpu.CompilerParams(
            dimension_semantics=("parallel", "parallel", "arbitrary"),
            vmem_limit_bytes=VMEM_LIMIT_BYTES),
        name="attn_c",
    )(proj, proj, proj, proj, proj, lq1.reshape(1, HEAD_DIM), lk1.reshape(1, HEAD_DIM),
      lq2.reshape(1, HEAD_DIM), lk2.reshape(1, HEAD_DIM), subln_g.reshape(1, hw))


def _out_proj_kernel(a_ref, b_ref, c_ref, w_ref, x_ref, g_ref, o_ref,
                     y_scr, ss_scr, *, n_row_tiles, n_col_tiles):
    i = pl.program_id(0)
    j = pl.program_id(1)
    cur = i % 2
    prev = 1 - cur

    def matmul():
        y = jnp.dot(a_ref[...], w_ref[0:A_Q, :], preferred_element_type=F32)
        y = y + jnp.dot(b_ref[...], w_ref[A_Q:A_Q + B_W, :], preferred_element_type=F32)
        y = y + jnp.dot(c_ref[...], w_ref[A_Q + B_W:D_MIX, :], preferred_element_type=F32)
        y_scr[cur, j] = y
        ss_scr[cur, j] = jnp.sum(y * y, axis=-1, keepdims=True)

    def finish():
        ss = ss_scr[prev, 0]
        for jj in range(1, n_col_tiles):
            ss = ss + ss_scr[prev, jj]
        inv = lax.rsqrt(ss * (1.0 / D_MODEL) + EPS)
        o_ref[...] = x_ref[...] + (y_scr[prev, j] * inv) * g_ref[...]

    pl.when(i == 0)(matmul)

    @pl.when(jnp.logical_and(i > 0, i < n_row_tiles))
    def _():
        finish()
        matmul()

    pl.when(i == n_row_tiles)(finish)


def _out_proj(mix_a, mix_b, mix_c, w_out_bf, layer, x2d, post_g):
    tokens = x2d.shape[0]
    tm, tn = OUT_TM, OUT_TN
    assert tokens % tm == 0 and D_MODEL % tn == 0
    n_row_tiles = tokens // tm
    n_col_tiles = D_MODEL // tn
    kernel = functools.partial(_out_proj_kernel, n_row_tiles=n_row_tiles, n_col_tiles=n_col_tiles)

    def lhs_row(i, j):
        return (jnp.minimum(i, n_row_tiles - 1), 0)

    def done_tile(i, j):
        return (jnp.maximum(i - 1, 0), jnp.where(i == 0, 0, j))

    return pl.pallas_call(
        kernel,
        grid=(n_row_tiles + 1, n_col_tiles),
        in_specs=[
            pl.BlockSpec((tm, A_Q), lhs_row),
            pl.BlockSpec((tm, B_W), lhs_row),
            pl.BlockSpec((tm, C_V), lhs_row),
            pl.BlockSpec((None, D_MIX, tn), lambda i, j: (layer, 0, j)),
            pl.BlockSpec((tm, tn), done_tile),
            pl.BlockSpec((1, tn), lambda i, j: (0, j)),
        ],
        out_specs=pl.BlockSpec((tm, tn), done_tile),
        out_shape=jax.ShapeDtypeStruct((tokens, D_MODEL), F32),
        scratch_shapes=[pltpu.VMEM((2, n_col_tiles, tm, tn), F32),
                        pltpu.VMEM((2, n_col_tiles, tm, 1), F32)],
        compiler_params=pltpu.CompilerParams(
            dimension_semantics=("arbitrary", "arbitrary"),
            vmem_limit_bytes=VMEM_LIMIT_BYTES),
        name="out_proj",
    )(mix_a, mix_b, mix_c, w_out_bf, x2d, post_g.reshape(1, D_MODEL))


def _angles(pos, dim, theta):
    inv = jnp.power(theta, -jnp.arange(0, dim, 2, dtype=F32) / dim)
    return pos.astype(F32)[:, None] * inv[None, :]


def _rope_tables(seq_len):
    t = jnp.arange(seq_len)
    ang_row = _angles(t // GRID_W, HEAD_DIM // 2, AXIAL_THETA)
    ang_col = _angles(t % GRID_W, HEAD_DIM // 2, AXIAL_THETA)
    cr, sr, cc_, sc_ = jnp.cos(ang_row), jnp.sin(ang_row), jnp.cos(ang_col), jnp.sin(ang_col)
    cos_ax = jnp.concatenate([cr, cr, cc_, cc_], axis=-1)
    sin_ax = jnp.concatenate([-sr, sr, -sc_, sc_], axis=-1)
    ang_t = _angles(t, ROPE_DIMS, ROPE_THETA)
    ct, st = jnp.cos(ang_t), jnp.sin(ang_t)
    rest = HEAD_DIM - ROPE_DIMS
    cos_p = jnp.concatenate([ct, ct, jnp.ones((seq_len, rest), F32)], axis=-1)
    sin_p = jnp.concatenate([-st, st, jnp.zeros((seq_len, rest), F32)], axis=-1)
    return cos_ax, sin_ax, cos_p, sin_p


def _trunk(x, params, w_in_bf, w_out_bf, bias_blocks):
    batch, seq_len, _ = x.shape
    tables = _rope_tables(seq_len)
    x2d = x.reshape(batch * seq_len, D_MODEL)
    depth = w_in_bf.shape[0]
    for l in range(depth):
        lam_init = 0.8 - 0.6 * math.exp(-0.3 * l)
        proj = _in_proj(x2d, params["pre_norm_g"][l], w_in_bf, l, params["a_q_norm_g"][l],
                        params["a_k_norm_g"][l], tables, seq_len)
        mix_a = _attn_a(proj, batch, seq_len)
        mix_b = _attn_b(proj, bias_blocks, l, batch, seq_len)
        mix_c = _attn_c(proj, params["c_lambda_q1"][l], params["c_lambda_k1"][l],
                        params["c_lambda_q2"][l], params["c_lambda_k2"][l],
                        params["c_subln_g"][l], lam_init, batch, seq_len)
        x2d = _out_proj(mix_a, mix_b, mix_c, w_out_bf, l, x2d, params["post_norm_g"][l])
    return x2d.reshape(batch, seq_len, D_MODEL)


def _prepare(params):
    w_in_bf = params["w_in"].astype(BF16)
    w_out_bf = params["w_out"].astype(BF16)
    return w_in_bf, w_out_bf, _bias_blocks(params["b_rel_bias"])


def kernel(x_prompt, x_sample, pre_norm_g, post_norm_g, w_in, w_out, a_q_norm_g, a_k_norm_g,
           b_rel_bias, c_lambda_q1, c_lambda_k1, c_lambda_q2, c_lambda_k2, c_subln_g):
    params = dict(pre_norm_g=pre_norm_g, post_norm_g=post_norm_g, a_q_norm_g=a_q_norm_g,
                  a_k_norm_g=a_k_norm_g, c_lambda_q1=c_lambda_q1, c_lambda_k1=c_lambda_k1,
                  c_lambda_q2=c_lambda_q2, c_lambda_k2=c_lambda_k2, c_subln_g=c_subln_g)
    params.update(w_in=w_in, w_out=w_out, b_rel_bias=b_rel_bias)
    prepared = _prepare(params)
    y_prompt = _trunk(x_prompt, params, *prepared)
    y_sample = _trunk(x_sample, params, *prepared)
    return (y_prompt, y_sample)
```

```python
import functools
import math

import jax
import jax.numpy as jnp
import numpy as np
from jax import lax
from jax.experimental import pallas as pl
from jax.experimental.pallas import tpu as pltpu

F32 = jnp.float32
BF16 = jnp.bfloat16

D_MODEL = 4096
HEAD_DIM = 128
GRID_W = 64
A_HEADS = 12
A_KV_HEADS = 4
A_GROUP = A_HEADS // A_KV_HEADS
B_HEADS = 8
C_HEADS = 6
A_Q = A_HEADS * HEAD_DIM
A_KV = A_KV_HEADS * HEAD_DIM
B_W = B_HEADS * HEAD_DIM
C_QK = C_HEADS * 2 * HEAD_DIM
C_V = C_HEADS * 2 * HEAD_DIM
D_MIX = A_Q + B_W + C_V
D_IN = A_Q + 2 * A_KV + 3 * B_W + 2 * C_QK + C_V + D_MIX
NA_ROWS = 8
NA_COLS = 16
AXIAL_THETA = 10000.0
ROPE_THETA = 500000.0
ROPE_DIMS = HEAD_DIM // 4
EPS = 1e-6
NEG_INF = -1e30
SCALE = 1.0 / math.sqrt(HEAD_DIM)
SCALE_LOG2 = SCALE * math.log2(math.e)
ONES_ROWS = 16

OFF_QA = 0
OFF_KA = OFF_QA + A_Q
OFF_VA = OFF_KA + A_KV
OFF_QB = OFF_VA + A_KV
OFF_KB = OFF_QB + B_W
OFF_VB = OFF_KB + B_W
OFF_QC = OFF_VB + B_W
OFF_KC = OFF_QC + C_QK
OFF_VC = OFF_KC + C_QK
OFF_GATE = OFF_VC + C_V

VMEM_LIMIT_BYTES = 56 * 1024 * 1024

IN_TM = 1024
IN_TN = 256
IN_SEG = 256
OUT_TM = 512
OUT_TN = 1024
A_TQ = 256
A_STREAMS = 2
A_TK = 512
B_TQ = 2048
B_QROWS = 4
B_WROWS = B_QROWS + NA_ROWS
B_CASES = 3
C_TQ = 512
C_TK = 512
KV_UNROLL = 16


def _swap_halves(x, h):
    lane = lax.broadcasted_iota(jnp.int32, x.shape, 1)
    first = (lane % (2 * h)) < h
    return jnp.where(first, pltpu.roll(x, HEAD_DIM - h, 1), pltpu.roll(x, h, 1))


def _silu(g):
    return g / (1.0 + jnp.exp(-g))


def _segment_kind(off):
    if off < OFF_KA:
        return "qa"
    if off < OFF_VA:
        return "ka"
    if OFF_QB <= off < OFF_KB:
        return "qb"
    if OFF_QC <= off < OFF_KC:
        return "qc"
    if OFF_KC <= off < OFF_VC:
        return "kc"
    return "plain"


def _in_tile_kinds(n_tiles):
    segs = IN_TN // IN_SEG
    return [tuple(_segment_kind((t * segs + s) * IN_SEG) for s in range(segs))
            for t in range(n_tiles)]


def _in_proj_kernel(x_ref, g_ref, w_ref, qn_ref, kn_ref, ca_ref, sa_ref, cc_ref, sc_ref,
                    o_ref, h_scr, acc_scr, *, n_tiles):
    j = pl.program_id(1)
    cur = j % 2
    prev = 1 - cur
    heads = IN_SEG // HEAD_DIM

    def pre_norm():
        x = x_ref[...]
        ms = jnp.sum(x * x, axis=-1, keepdims=True) * (1.0 / D_MODEL)
        h_scr[...] = ((x * lax.rsqrt(ms + EPS)) * g_ref[...]).astype(BF16)

    def matmul():
        acc_scr[cur] = jnp.dot(h_scr[...], w_ref[...], preferred_element_type=F32)

    def first_step():
        pre_norm()
        matmul()

    def head_norm(y, g):
        ms = jnp.sum(y * y, axis=-1, keepdims=True) * (1.0 / HEAD_DIM)
        return (y * lax.rsqrt(ms + EPS)) * g

    def axial(y):
        return y * ca_ref[...] + _swap_halves(y, HEAD_DIM // 4) * sa_ref[...]

    def partial(y):
        return y * cc_ref[...] + _swap_halves(y, ROPE_DIMS // 2) * sc_ref[...]

    epilogues = {
        "qa": lambda y: axial(head_norm(y, qn_ref[...])) * SCALE_LOG2,
        "ka": lambda y: axial(head_norm(y, kn_ref[...])),
        "qb": lambda y: y * SCALE_LOG2,
        "qc": lambda y: partial(y) * SCALE_LOG2,
        "kc": partial,
        "plain": lambda y: y,
    }

    def finish(kinds):
        for seg, kind in enumerate(kinds):
            for hh in range(heads):
                lo = seg * IN_SEG + hh * HEAD_DIM
                o_ref[:, lo:lo + HEAD_DIM] = epilogues[kind](
                    acc_scr[prev, :, lo:lo + HEAD_DIM]).astype(BF16)

    def step(kinds):
        finish(kinds)
        matmul()

    tile_kinds = _in_tile_kinds(n_tiles)
    pl.when(j == 0)(first_step)
    for kinds in sorted(set(tile_kinds[:-1])):
        tiles = [t for t in range(n_tiles - 1) if tile_kinds[t] == kinds]
        cond = functools.reduce(jnp.logical_or, [j == t + 1 for t in tiles])
        pl.when(cond)(functools.partial(step, kinds))
    pl.when(j == n_tiles)(functools.partial(finish, tile_kinds[-1]))


def _in_proj(x2d, pre_g, w_in_bf, layer, qn_g, kn_g, tables, seq_len):
    tokens = x2d.shape[0]
    tm = IN_TM
    assert tokens % tm == 0 and seq_len % tm == 0 and D_IN % IN_TN == 0 and IN_TN % IN_SEG == 0
    for off in (OFF_KA, OFF_VA, OFF_QB, OFF_KB, OFF_QC, OFF_KC, OFF_VC):
        assert off % IN_SEG == 0
    n_tiles = D_IN // IN_TN
    pos_blocks = seq_len // tm
    tab_spec = pl.BlockSpec((tm, HEAD_DIM), lambda i, j: (i % pos_blocks, 0))
    vec_spec = pl.BlockSpec((1, HEAD_DIM), lambda i, j: (0, 0))
    return pl.pallas_call(
        functools.partial(_in_proj_kernel, n_tiles=n_tiles),
        grid=(tokens // tm, n_tiles + 1),
        in_specs=[
            pl.BlockSpec((tm, D_MODEL), lambda i, j: (i, 0)),
            pl.BlockSpec((1, D_MODEL), lambda i, j: (0, 0)),
            pl.BlockSpec((None, D_MODEL, IN_TN),
                         lambda i, j: (layer, 0, jnp.minimum(j, n_tiles - 1))),
            vec_spec, vec_spec, tab_spec, tab_spec, tab_spec, tab_spec,
        ],
        out_specs=pl.BlockSpec((tm, IN_TN), lambda i, j: (i, jnp.maximum(j - 1, 0))),
        out_shape=jax.ShapeDtypeStruct((tokens, D_IN), BF16),
        scratch_shapes=[pltpu.VMEM((tm, D_MODEL), BF16), pltpu.VMEM((2, tm, IN_TN), F32)],
        compiler_params=pltpu.CompilerParams(
            dimension_semantics=("parallel", "arbitrary"),
            vmem_limit_bytes=VMEM_LIMIT_BYTES),
        name="in_proj",
    )(x2d, pre_g.reshape(1, D_MODEL), w_in_bf, qn_g.reshape(1, HEAD_DIM),
      kn_g.reshape(1, HEAD_DIM), *tables)


def _build_vt(v_ref, vt_scr, *, n_kv, tk, dv):
    def chunk(c, carry):
        off = pl.multiple_of(c * tk, tk)
        vt_scr[c, 0:dv, :] = v_ref[pl.ds(off, tk), :].astype(F32).T.astype(BF16)
        row = lax.broadcasted_iota(jnp.int32, (ONES_ROWS, tk), 0)
        vt_scr[c, dv:dv + ONES_ROWS, :] = jnp.where(row == 0, 1.0, 0.0).astype(BF16)
        return carry

    lax.fori_loop(0, n_kv, chunk, 0)


def _kv_unroll(n_kv, cap):
    unroll = min(cap, n_kv)
    assert unroll % 2 == 0 and n_kv % unroll == 0
    return unroll


def _flash_step(s, vt, m_scr, acc_scr):
    m_old = m_scr[...]
    m_new = jnp.maximum(m_old, jnp.max(s, axis=0, keepdims=True))
    alpha = jnp.exp2(m_old - m_new)
    p = jnp.exp2(s - m_new).astype(BF16)
    acc_scr[...] = alpha * acc_scr[...] + jnp.dot(vt, p, preferred_element_type=F32)
    m_scr[...] = m_new


def _attn_a_kernel(q_ref, qn_ref, k_ref, v_ref, g0_ref, g1_ref, g2_ref, o_ref,
                   qt_scr, vt_scr, s_scr, m_scr, acc_scr, *, tq, tk, n_kv):
    gate_refs = (g0_ref, g1_ref, g2_ref)
    i = pl.program_id(2)
    cur = i % 2
    assert n_kv % 2 == 0
    streams = range(A_STREAMS)

    def load_qt(ref, slot, st):
        for g in range(A_GROUP):
            qt_scr[slot, st, :, g * tq:(g + 1) * tq] = (
                ref[st * tq:(st + 1) * tq,
                    g * HEAD_DIM:(g + 1) * HEAD_DIM].astype(F32).T.astype(BF16))

    def scores(j, slot, st):
        start = pl.multiple_of(j * tk, tk)
        return jnp.dot(k_ref[pl.ds(start, tk), :], qt_scr[slot, st],
                       preferred_element_type=F32)

    @pl.when(i == 0)
    def _():
        _build_vt(v_ref, vt_scr, n_kv=n_kv, tk=tk, dv=HEAD_DIM)
        for st in streams:
            load_qt(q_ref, 0, st)
            s_scr[0, st] = scores(0, 0, st)

    m_scr[...] = jnp.full(m_scr.shape, NEG_INF, F32)
    acc_scr[...] = jnp.zeros(acc_scr.shape, F32)

    unroll = _kv_unroll(n_kv, KV_UNROLL // A_STREAMS)

    def step_group(jj, is_tail):
        for u in range(unroll):
            j = unroll * jj + u
            slot = u % 2
            for st in streams:
                if is_tail and u == unroll - 1:
                    load_qt(qn_ref, 1 - cur, st)
                    s_scr[0, st] = scores(0, 1 - cur, st)
                else:
                    s_scr[1 - slot, st] = scores(j + 1, cur, st)
                _flash_step(s_scr[slot, st], vt_scr[j], m_scr.at[st], acc_scr.at[st])

    def body(jj, carry):
        step_group(jj, False)
        return carry

    lax.fori_loop(0, n_kv // unroll - 1, body, 0)
    step_group(n_kv // unroll - 1, True)
    for st in streams:
        acc = acc_scr[st]
        out = (acc[0:HEAD_DIM] / acc[HEAD_DIM:HEAD_DIM + 1]).T
        rows = slice(st * tq, (st + 1) * tq)
        for g in range(A_GROUP):
            sl = slice(g * HEAD_DIM, (g + 1) * HEAD_DIM)
            gate = gate_refs[g][rows, :].astype(F32)
            o_ref[rows, sl] = (out[g * tq:(g + 1) * tq, :] * _silu(gate)).astype(BF16)


def _attn_a(proj, batch, seq_len):
    tq, tk = A_TQ, A_TK
    tile = A_STREAMS * tq
    assert seq_len % tile == 0 and seq_len % tk == 0
    nq = seq_len // tile
    n_kv = seq_len // tk
    gw = A_GROUP * HEAD_DIM
    kernel = functools.partial(_attn_a_kernel, tq=tq, tk=tk, n_kv=n_kv)

    def gate_spec(g):
        return pl.BlockSpec(
            (tile, HEAD_DIM),
            lambda b, h, i: (b * nq + i, OFF_GATE // HEAD_DIM + A_GROUP * h + g))

    return pl.pallas_call(
        kernel,
        grid=(batch, A_KV_HEADS, nq),
        in_specs=[
            pl.BlockSpec((tile, gw), lambda b, h, i: (b * nq + i, h)),
            pl.BlockSpec((tile, gw), lambda b, h, i: (b * nq + jnp.minimum(i + 1, nq - 1), h)),
            pl.BlockSpec((seq_len, HEAD_DIM), lambda b, h, i: (b, OFF_KA // HEAD_DIM + h)),
            pl.BlockSpec((seq_len, HEAD_DIM), lambda b, h, i: (b, OFF_VA // HEAD_DIM + h)),
            gate_spec(0), gate_spec(1), gate_spec(2),
        ],
        out_specs=pl.BlockSpec((tile, gw), lambda b, h, i: (b * nq + i, h)),
        out_shape=jax.ShapeDtypeStruct((batch * seq_len, A_Q), BF16),
        scratch_shapes=[
            pltpu.VMEM((2, A_STREAMS, HEAD_DIM, A_GROUP * tq), BF16),
            pltpu.VMEM((n_kv, HEAD_DIM + ONES_ROWS, tk), BF16),
            pltpu.VMEM((2, A_STREAMS, tk, A_GROUP * tq), F32),
            pltpu.VMEM((A_STREAMS, 1, A_GROUP * tq), F32),
            pltpu.VMEM((A_STREAMS, HEAD_DIM + ONES_ROWS, A_GROUP * tq), F32),
        ],
        compiler_params=pltpu.CompilerParams(
            dimension_semantics=("parallel", "parallel", "arbitrary"),
            vmem_limit_bytes=VMEM_LIMIT_BYTES),
        name="attn_a",
    )(proj, proj, proj, proj, proj, proj, proj)


def _attn_b_kernel(q_ref, k_ref, v_ref, gate_ref, bias_ref, o_ref, vt_scr,
                   *, blocks_per_step, grid_rows):
    t = pl.program_id(2)
    bq = B_QROWS * GRID_W
    wk = B_WROWS * GRID_W
    n_blocks = grid_rows // B_QROWS

    @pl.when(t == 0)
    def _():
        _build_vt(v_ref, vt_scr, n_kv=n_blocks, tk=bq, dv=HEAD_DIM)

    def block_rows(u):
        return slice(u * bq, (u + 1) * bq)

    def window(u):
        blk = t * blocks_per_step + u
        win_row = jnp.clip(blk * B_QROWS - NA_ROWS // 2, 0, grid_rows - B_WROWS)
        case = jnp.where(blk == 0, 0, jnp.where(blk == n_blocks - 1, 2, 1))
        return win_row, case

    def scores(u):
        win_row, case = window(u)
        qt = q_ref[block_rows(u), :].astype(F32).T.astype(BF16)
        k = k_ref[pl.ds(pl.multiple_of(win_row * GRID_W, bq), wk), :]
        return jnp.dot(k, qt, preferred_element_type=F32) + bias_ref[case]

    def softmax(s):
        return jnp.exp2(s - jnp.max(s, axis=0, keepdims=True)).astype(BF16)

    def weighted_values(u, p):
        chunk0 = window(u)[0] // B_QROWS
        vt = jnp.concatenate([vt_scr[chunk0 + w] for w in range(B_WROWS // B_QROWS)], axis=1)
        return jnp.dot(vt, p, preferred_element_type=F32)

    def store(u, acc):
        o = (acc[0:HEAD_DIM] / acc[HEAD_DIM:HEAD_DIM + 1]).T
        gate = gate_ref[block_rows(u), :].astype(F32)
        o_ref[block_rows(u), :] = (o * _silu(gate)).astype(BF16)

    n = blocks_per_step
    s, p, acc = {0: scores(0)}, {}, {}
    for u in range(n + 2):
        if u + 1 < n:
            s[u + 1] = scores(u + 1)
        if 0 <= u - 1 < n:
            acc[u - 1] = weighted_values(u - 1, p.pop(u - 1))
        if u < n:
            p[u] = softmax(s.pop(u))
        if 0 <= u - 2 < n:
            store(u - 2, acc.pop(u - 2))


def _attn_b(proj, bias_blocks, layer, batch, seq_len):
    tq = min(B_TQ, seq_len)
    bq = B_QROWS * GRID_W
    assert seq_len % tq == 0 and tq % bq == 0
    grid_rows = seq_len // GRID_W
    assert grid_rows % B_QROWS == 0 and grid_rows >= 4 * B_QROWS
    nq = seq_len // tq
    n_blocks = grid_rows // B_QROWS
    kernel = functools.partial(_attn_b_kernel, blocks_per_step=tq // bq, grid_rows=grid_rows)
    return pl.pallas_call(
        kernel,
        grid=(batch, B_HEADS, nq),
        in_specs=[
            pl.BlockSpec((tq, HEAD_DIM), lambda b, h, i: (b * nq + i, OFF_QB // HEAD_DIM + h)),
            pl.BlockSpec((seq_len, HEAD_DIM), lambda b, h, i: (b, OFF_KB // HEAD_DIM + h)),
            pl.BlockSpec((seq_len, HEAD_DIM), lambda b, h, i: (b, OFF_VB // HEAD_DIM + h)),
            pl.BlockSpec((tq, HEAD_DIM),
                         lambda b, h, i: (b * nq + i, (OFF_GATE + A_Q) // HEAD_DIM + h)),
            pl.BlockSpec((None, None, B_CASES, B_WROWS * GRID_W, bq),
                         lambda b, h, i: (layer, h, 0, 0, 0)),
        ],
        out_specs=pl.BlockSpec((tq, HEAD_DIM), lambda b, h, i: (b * nq + i, h)),
        out_shape=jax.ShapeDtypeStruct((batch * seq_len, B_W), BF16),
        scratch_shapes=[pltpu.VMEM((n_blocks, HEAD_DIM + ONES_ROWS, bq), BF16)],
        compiler_params=pltpu.CompilerParams(
            dimension_semantics=("parallel", "parallel", "arbitrary"),
            vmem_limit_bytes=VMEM_LIMIT_BYTES),
        name="attn_b",
    )(proj, proj, proj, proj, bias_blocks)


def _bias_blocks(rel_bias):
    layers, heads = rel_bias.shape[:2]
    n_dr, n_dc = 2 * NA_ROWS - 1, 2 * NA_COLS - 1
    c = np.arange(GRID_W)
    col_start = np.clip(c - NA_COLS // 2, 0, GRID_W - NA_COLS)
    col_ok = (c[None, :] >= col_start[:, None]) & (c[None, :] < col_start[:, None] + NA_COLS)
    dc = np.clip(c[None, :] - c[:, None], -(NA_COLS - 1), NA_COLS - 1) + (NA_COLS - 1)
    any_rows = 8 * B_QROWS
    pick_dr, row_ok = [], []
    for first_row, win_row in ((0, 0), (2 * B_QROWS, B_QROWS), (any_rows - B_QROWS,
                                                                 any_rows - B_WROWS)):
        r = first_row + np.arange(B_QROWS)
        kr = win_row + np.arange(B_WROWS)
        band0 = np.clip(r - NA_ROWS // 2, 0, any_rows - NA_ROWS)
        row_ok.append((kr[None, :] >= band0[:, None]) & (kr[None, :] < band0[:, None] + NA_ROWS))
        dr = np.clip(kr[None, :] - r[:, None] + (NA_ROWS - 1), 0, n_dr - 1)
        pick_dr.append(dr.reshape(-1, 1) == np.arange(n_dr)[None, :])
    pick_dr = np.stack(pick_dr).astype(np.float32)
    pick_dc = (dc.reshape(-1, 1) == np.arange(n_dc)[None, :]).astype(np.float32)
    bias = jnp.einsum("lhij,cpi,qj->lhcpq", rel_bias.astype(F32), pick_dr, pick_dc,
                      precision=lax.Precision.HIGHEST)
    bias = bias.reshape(layers, heads, B_CASES, B_QROWS, B_WROWS, GRID_W, GRID_W)
    ok = np.stack(row_ok)[:, :, :, None, None] & col_ok[None, None, None, :, :]
    bias = jnp.where(ok[None, None], bias * math.log2(math.e), NEG_INF)
    return jnp.transpose(bias, (0, 1, 2, 4, 6, 3, 5)).reshape(
        layers, heads, B_CASES, B_WROWS * GRID_W, B_QROWS * GRID_W)


def _attn_c_kernel(q_ref, qn_ref, k_ref, v_ref, gate_ref, lq1_ref, lk1_ref, lq2_ref, lk2_ref,
                   sg_ref, o_ref, qt_scr, vt_scr, s_scr, m1_scr, a1_scr, m2_scr, a2_scr,
                   *, tk, n_kv, lam_init):
    dv = 2 * HEAD_DIM
    streams = ((m1_scr, a1_scr), (m2_scr, a2_scr))
    i = pl.program_id(2)
    cur = i % 2
    assert n_kv % 2 == 0

    def load_qt(ref, slot, u):
        qt_scr[slot, u] = ref[:, u * HEAD_DIM:(u + 1) * HEAD_DIM].astype(F32).T.astype(BF16)

    def scores(j, slot, u):
        start = pl.multiple_of(j * tk, tk)
        k = k_ref[pl.ds(start, tk), u * HEAD_DIM:(u + 1) * HEAD_DIM]
        return jnp.dot(k, qt_scr[slot, u], preferred_element_type=F32)

    @pl.when(i == 0)
    def _():
        _build_vt(v_ref, vt_scr, n_kv=n_kv, tk=tk, dv=dv)
        for u in range(2):
            load_qt(q_ref, 0, u)
            s_scr[0, u] = scores(0, 0, u)

    for m_scr, a_scr in streams:
        m_scr[...] = jnp.full(m_scr.shape, NEG_INF, F32)
        a_scr[...] = jnp.zeros(a_scr.shape, F32)

    unroll = _kv_unroll(n_kv, KV_UNROLL // 2)

    def step_group(jj, is_tail):
        for t in range(unroll):
            j = unroll * jj + t
            slot = t % 2
            for u, (m_scr, a_scr) in enumerate(streams):
                if is_tail and t == unroll - 1:
                    load_qt(qn_ref, 1 - cur, u)
                    s_scr[0, u] = scores(0, 1 - cur, u)
                else:
                    s_scr[1 - slot, u] = scores(j + 1, cur, u)
                _flash_step(s_scr[slot, u], vt_scr[j], m_scr, a_scr)

    def body(jj, carry):
        step_group(jj, False)
        return carry

    lax.fori_loop(0, n_kv // unroll - 1, body, 0)
    step_group(n_kv // unroll - 1, True)

    lam = (jnp.exp(jnp.sum(lq1_ref[...] * lk1_ref[...], axis=-1, keepdims=True))
           - jnp.exp(jnp.sum(lq2_ref[...] * lk2_ref[...], axis=-1, keepdims=True))
           + lam_init)
    a1 = a1_scr[...]
    a2 = a2_scr[...]
    o = (a1[0:dv] / a1[dv:dv + 1] - lam * (a2[0:dv] / a2[dv:dv + 1])).T
    ms = jnp.sum(o * o, axis=-1, keepdims=True) * (1.0 / (2 * HEAD_DIM))
    o = (o * lax.rsqrt(ms + EPS)) * sg_ref[...] * (1.0 - lam_init)
    o_ref[...] = (o * _silu(gate_ref[...].astype(F32))).astype(BF16)


def _attn_c(proj, lq1, lk1, lq2, lk2, subln_g, lam_init, batch, seq_len):
    tq, tk = C_TQ, C_TK
    assert seq_len % tq == 0 and seq_len % tk == 0
    nq = seq_len // tq
    n_kv = seq_len // tk
    hw = 2 * HEAD_DIM
    kernel = functools.partial(_attn_c_kernel, tk=tk, n_kv=n_kv, lam_init=lam_init)
    vec = pl.BlockSpec((1, HEAD_DIM), lambda b, h, i: (0, 0))
    return pl.pallas_call(
        kernel,
        grid=(batch, C_HEADS, nq),
        in_specs=[
            pl.BlockSpec((tq, hw), lambda b, h, i: (b * nq + i, OFF_QC // hw + h)),
            pl.BlockSpec((tq, hw),
                         lambda b, h, i: (b * nq + jnp.minimum(i + 1, nq - 1), OFF_QC // hw + h)),
            pl.BlockSpec((seq_len, hw), lambda b, h, i: (b, OFF_KC // hw + h)),
            pl.BlockSpec((seq_len, hw), lambda b, h, i: (b, OFF_VC // hw + h)),
            pl.BlockSpec((tq, hw), lambda b, h, i: (b * nq + i, (OFF_GATE + A_Q + B_W) // hw + h)),
            vec, vec, vec, vec,
            pl.BlockSpec((1, hw), lambda b, h, i: (0, 0)),
        ],
        out_specs=pl.BlockSpec((tq, hw), lambda b, h, i: (b * nq + i, h)),
        out_shape=jax.ShapeDtypeStruct((batch * seq_len, C_V), BF16),
        scratch_shapes=[
            pltpu.VMEM((2, 2, HEAD_DIM, tq), BF16),
            pltpu.VMEM((n_kv, hw + ONES_ROWS, tk), BF16),
            pltpu.VMEM((2, 2, tk, tq), F32),
            pltpu.VMEM((1, tq), F32), pltpu.VMEM((hw + ONES_ROWS, tq), F32),
            pltpu.VMEM((1, tq), F32), pltpu.VMEM((hw + ONES_ROWS, tq), F32),
        ],
        compiler_params=pltpu.CompilerParams(
            dimension_semantics=("parallel", "parallel", "arbitrary"),
            vmem_limit_bytes=VMEM_LIMIT_BYTES),
        name="attn_c",
    )(proj, proj, proj, proj, proj, lq1.reshape(1, HEAD_DIM), lk1.reshape(1, HEAD_DIM),
      lq2.reshape(1, HEAD_DIM), lk2.reshape(1, HEAD_DIM), subln_g.reshape(1, hw))


def _out_proj_kernel(a_ref, b_ref, c_ref, w_ref, x_ref, g_ref, o_ref,
                     y_scr, ss_scr, *, n_row_tiles, n_col_tiles):
    i = pl.program_id(0)
    j = pl.program_id(1)
    cur = i % 2
    prev = 1 - cur

    def matmul():
        y = jnp.dot(a_ref[...], w_ref[0:A_Q, :], preferred_element_type=F32)
        y = y + jnp.dot(b_ref[...], w_ref[A_Q:A_Q + B_W, :], preferred_element_type=F32)
        y = y + jnp.dot(c_ref[...], w_ref[A_Q + B_W:D_MIX, :], preferred_element_type=F32)
        y_scr[cur, j] = y
        ss_scr[cur, j] = jnp.sum(y * y, axis=-1, keepdims=True)

    def finish():
        ss = ss_scr[prev, 0]
        for jj in range(1, n_col_tiles):
            ss = ss + ss_scr[prev, jj]
        inv = lax.rsqrt(ss * (1.0 / D_MODEL) + EPS)
        o_ref[...] = x_ref[...] + (y_scr[prev, j] * inv) * g_ref[...]

    pl.when(i == 0)(matmul)

    @pl.when(jnp.logical_and(i > 0, i < n_row_tiles))
    def _():
        finish()
        matmul()

    pl.when(i == n_row_tiles)(finish)


def _out_proj(mix_a, mix_b, mix_c, w_out_bf, layer, x2d, post_g):
    tokens = x2d.shape[0]
    tm, tn = OUT_TM, OUT_TN
    assert tokens % tm == 0 and D_MODEL % tn == 0
    n_row_tiles = tokens // tm
    n_col_tiles = D_MODEL // tn
    kernel = functools.partial(_out_proj_kernel, n_row_tiles=n_row_tiles, n_col_tiles=n_col_tiles)

    def lhs_row(i, j):
        return (jnp.minimum(i, n_row_tiles - 1), 0)

    def done_tile(i, j):
        return (jnp.maximum(i - 1, 0), jnp.where(i == 0, 0, j))

    return pl.pallas_call(
        kernel,
        grid=(n_row_tiles + 1, n_col_tiles),
        in_specs=[
            pl.BlockSpec((tm, A_Q), lhs_row),
            pl.BlockSpec((tm, B_W), lhs_row),
            pl.BlockSpec((tm, C_V), lhs_row),
            pl.BlockSpec((None, D_MIX, tn), lambda i, j: (layer, 0, j)),
            pl.BlockSpec((tm, tn), done_tile),
            pl.BlockSpec((1, tn), lambda i, j: (0, j)),
        ],
        out_specs=pl.BlockSpec((tm, tn), done_tile),
        out_shape=jax.ShapeDtypeStruct((tokens, D_MODEL), F32),
        scratch_shapes=[pltpu.VMEM((2, n_col_tiles, tm, tn), F32),
                        pltpu.VMEM((2, n_col_tiles, tm, 1), F32)],
        compiler_params=pltpu.CompilerParams(
            dimension_semantics=("arbitrary", "arbitrary"),
            vmem_limit_bytes=VMEM_LIMIT_BYTES),
        name="out_proj",
    )(mix_a, mix_b, mix_c, w_out_bf, x2d, post_g.reshape(1, D_MODEL))


def _angles(pos, dim, theta):
    inv = jnp.power(theta, -jnp.arange(0, dim, 2, dtype=F32) / dim)
    return pos.astype(F32)[:, None] * inv[None, :]


def _rope_tables(seq_len):
    t = jnp.arange(seq_len)
    ang_row = _angles(t // GRID_W, HEAD_DIM // 2, AXIAL_THETA)
    ang_col = _angles(t % GRID_W, HEAD_DIM // 2, AXIAL_THETA)
    cr, sr, cc_, sc_ = jnp.cos(ang_row), jnp.sin(ang_row), jnp.cos(ang_col), jnp.sin(ang_col)
    cos_ax = jnp.concatenate([cr, cr, cc_, cc_], axis=-1)
    sin_ax = jnp.concatenate([-sr, sr, -sc_, sc_], axis=-1)
    ang_t = _angles(t, ROPE_DIMS, ROPE_THETA)
    ct, st = jnp.cos(ang_t), jnp.sin(ang_t)
    rest = HEAD_DIM - ROPE_DIMS
    cos_p = jnp.concatenate([ct, ct, jnp.ones((seq_len, rest), F32)], axis=-1)
    sin_p = jnp.concatenate([-st, st, jnp.zeros((seq_len, rest), F32)], axis=-1)
    return cos_ax, sin_ax, cos_p, sin_p


def _trunk(x, params, w_in_bf, w_out_bf, bias_blocks):
    batch, seq_len, _ = x.shape
    tables = _rope_tables(seq_len)
    x2d = x.reshape(batch * seq_len, D_MODEL)
    depth = w_in_bf.shape[0]
    for l in range(depth):
        lam_init = 0.8 - 0.6 * math.exp(-0.3 * l)
        proj = _in_proj(x2d, params["pre_norm_g"][l], w_in_bf, l, params["a_q_norm_g"][l],
                        params["a_k_norm_g"][l], tables, seq_len)
        mix_a = _attn_a(proj, batch, seq_len)
        mix_b = _attn_b(proj, bias_blocks, l, batch, seq_len)
        mix_c = _attn_c(proj, params["c_lambda_q1"][l], params["c_lambda_k1"][l],
                        params["c_lambda_q2"][l], params["c_lambda_k2"][l],
                        params["c_subln_g"][l], lam_init, batch, seq_len)
        x2d = _out_proj(mix_a, mix_b, mix_c, w_out_bf, l, x2d, params["post_norm_g"][l])
    return x2d.reshape(batch, seq_len, D_MODEL)


def _prepare(params):
    w_in_bf = params["w_in"].astype(BF16)
    w_out_bf = params["w_out"].astype(BF16)
    return w_in_bf, w_out_bf, _bias_blocks(params["b_rel_bias"])


def kernel(x_prompt, x_sample, pre_norm_g, post_norm_g, w_in, w_out, a_q_norm_g, a_k_norm_g,
           b_rel_bias, c_lambda_q1, c_lambda_k1, c_lambda_q2, c_lambda_k2, c_subln_g):
    params = dict(pre_norm_g=pre_norm_g, post_norm_g=post_norm_g, a_q_norm_g=a_q_norm_g,
                  a_k_norm_g=a_k_norm_g, c_lambda_q1=c_lambda_q1, c_lambda_k1=c_lambda_k1,
                  c_lambda_q2=c_lambda_q2, c_lambda_k2=c_lambda_k2, c_subln_g=c_subln_g)
    params.update(w_in=w_in, w_out=w_out, b_rel_bias=b_rel_bias)
    prepared = _prepare(params)
    y_prompt = _trunk(x_prompt, params, *prepared)
    y_sample = _trunk(x_sample, params, *prepared)
    return (y_prompt, y_sample)
```
